```python
import math
import jax
import jax.numpy as jnp
from jax import lax
import numpy as np

D_MODEL = 1024
BATCH = 8
SEQ = 4096
DEPTH = 2
DEC_BATCH = 32
DEC_SEQ = 8
PAST_LEN = 16384
PAGE_SIZE = 128

N_MIXERS = 2
N_A_LAYERS = (DEPTH + N_MIXERS - 1) // N_MIXERS
N_B_LAYERS = DEPTH // N_MIXERS
DN_HEADS = 8
DN_DK = 128
DN_DV = 128
DN_QK = DN_HEADS * DN_DK
DN_VW = DN_HEADS * DN_DV
DN_CONV = 4
DN_CONV_CH = 2 * DN_QK + DN_VW
DN_PROJ = DN_CONV_CH + DN_VW + 2 * DN_HEADS
DN_CHUNK = 64
MOBA_HEADS = 8
MOBA_DH = 128
MOBA_W = MOBA_HEADS * MOBA_DH
MOBA_BLOCK = 256
MOBA_TOPK = 3
MOBA_QB = 32
PPB = MOBA_BLOCK // PAGE_SIZE
MEM_LEN = 256
XA_HEADS = 4
XA_DH = 128
XA_W = XA_HEADS * XA_DH
D_FF = 2816
FFN_CONV = 3
EPS = 1e-6
F32 = jnp.float32

kernel_name = "hybrid_deltanet_moba_decoder_step"


def rmsnorm(x, gain):
    xf = x.astype(F32)
    y = xf * lax.rsqrt(jnp.mean(xf * xf, axis=-1, keepdims=True) + EPS)
    return (y * gain.astype(F32)).astype(x.dtype)


def l2norm(x):
    return x * lax.rsqrt(jnp.sum(x * x, axis=-1, keepdims=True) + EPS)


def causal_depthwise_conv(x, buf, w):
    width = w.shape[0]
    t = x.shape[1]
    xp = jnp.concatenate([buf.astype(x.dtype), x], axis=1)
    y = xp[:, 0:t] * w[0]
    for j in range(1, width):
        y = y + xp[:, j:j + t] * w[j]
    return y, xp[:, xp.shape[1] - (width - 1):]


def alibi_slopes(n_heads):
    return jnp.exp2(-8.0 * jnp.arange(1, n_heads + 1, dtype=F32) / n_heads)


def gated_delta_chunked(q, k, v, beta, g, s0, chunk):
    b, t, h, _ = q.shape
    n = t // chunk

    def to_chunks(a):
        a = a.reshape((b, n, chunk) + a.shape[2:])
        return jnp.moveaxis(a, (1, 3), (0, 2))

    q, k, v, beta, g = (to_chunks(a) for a in (q, k, v, beta, g))
    gc = jnp.cumsum(g, axis=-1)
    pos = jnp.arange(chunk)
    causal = pos[:, None] >= pos[None, :]
    strict = pos[:, None] > pos[None, :]
    gamma = jnp.exp(jnp.where(causal, gc[..., :, None] - gc[..., None, :], -jnp.inf))
    kb = k * beta[..., None]
    a_low = jnp.where(strict, jnp.einsum('nbhck,nbhsk->nbhcs', kb, k) * gamma, 0.0)
    eye = jnp.eye(chunk, dtype=F32)
    t_inv = lax.linalg.triangular_solve(a_low + eye, jnp.broadcast_to(eye, a_low.shape),
                                        left_side=True, lower=True)
    u = t_inv @ (v * beta[..., None])
    w = t_inv @ (kb * jnp.exp(gc)[..., None])
    qk = jnp.where(causal, jnp.einsum('nbhck,nbhsk->nbhcs', q, k) * gamma, 0.0)
    q_dec = q * jnp.exp(gc)[..., None]
    k_dec = k * jnp.exp(gc[..., -1:] - gc)[..., None]
    g_last = jnp.exp(gc[..., -1])

    def step(s, xs):
        w_i, u_i, qd_i, qk_i, kd_i, gl_i = xs
        v_new = u_i - jnp.einsum('bhck,bhkv->bhcv', w_i, s)
        o_i = jnp.einsum('bhck,bhkv->bhcv', qd_i, s) + jnp.einsum('bhcs,bhsv->bhcv', qk_i, v_new)
        s = s * gl_i[..., None, None] + jnp.einsum('bhck,bhcv->bhkv', kd_i, v_new)
        return s, o_i

    s_fin, o = lax.scan(step, s0, (w, u, q_dec, qk, k_dec, g_last))
    o = jnp.moveaxis(o, (0, 2), (1, 3)).reshape(b, t, h, -1)
    return o, s_fin


def deltanet_mixer(xn, s0, conv_buf, w_in, conv_w, a_log, dt_bias, o_gain, w_out):
    b, t, _ = xn.shape
    proj = xn @ w_in
    qkv = proj[..., :DN_CONV_CH]
    z = proj[..., DN_CONV_CH:DN_CONV_CH + DN_VW]
    beta_in = proj[..., DN_CONV_CH + DN_VW:DN_CONV_CH + DN_VW + DN_HEADS]
    a_in = proj[..., DN_CONV_CH + DN_VW + DN_HEADS:]
    qkv, new_buf = causal_depthwise_conv(qkv, conv_buf, conv_w)
    qkv = jax.nn.silu(qkv.astype(F32))
    q = l2norm(qkv[..., :DN_QK].reshape(b, t, DN_HEADS, DN_DK)) * DN_DK ** -0.5
    k = l2norm(qkv[..., DN_QK:2 * DN_QK].reshape(b, t, DN_HEADS, DN_DK))
    v = qkv[..., 2 * DN_QK:].reshape(b, t, DN_HEADS, DN_DV)
    beta = jax.nn.sigmoid(beta_in.astype(F32))
    g = -jnp.exp(a_log.astype(F32)) * jax.nn.softplus(a_in.astype(F32) + dt_bias.astype(F32))
    chunk = DN_CHUNK if t % DN_CHUNK == 0 else t
    o, s_new = gated_delta_chunked(q, k, v, beta, g, s0.astype(F32), chunk)
    o = o * lax.rsqrt(jnp.mean(o * o, axis=-1, keepdims=True) + EPS) * o_gain.astype(F32)
    o = o * jax.nn.silu(z.astype(F32)).reshape(b, t, DN_HEADS, DN_DV)
    return o.reshape(b, t, DN_VW).astype(xn.dtype) @ w_out, s_new, new_buf


def moba_prompt(xn, w_qkv, w_out):
    b, t, _ = xn.shape
    qkv = (xn @ w_qkv).reshape(b, t, 3, MOBA_HEADS, MOBA_DH)
    k_rows, v_rows = qkv[:, :, 1], qkv[:, :, 2]
    nb = -(-t // MOBA_BLOCK)
    tp = nb * MOBA_BLOCK
    pad = ((0, 0), (0, tp - t), (0, 0), (0, 0))
    qh = jnp.pad(qkv[:, :, 0], pad).transpose(0, 2, 1, 3) * MOBA_DH ** -0.5
    kb = jnp.pad(k_rows, pad).transpose(0, 2, 1, 3).reshape(b, MOBA_HEADS, nb, MOBA_BLOCK, MOBA_DH)
    vb = jnp.pad(v_rows, pad).transpose(0, 2, 1, 3).reshape(b, MOBA_HEADS, nb, MOBA_BLOCK, MOBA_DH)
    slopes = alibi_slopes(MOBA_HEADS)
    n_sel = min(MOBA_TOPK, nb - 1)
    nqb = tp // MOBA_QB

    def fold(a):
        a5 = a.reshape((b, MOBA_HEADS, nqb, MOBA_QB) + a.shape[3:])
        return jnp.moveaxis(a5, 2, 1).reshape((b * nqb, MOBA_HEADS, MOBA_QB) + a5.shape[4:])

    xs = (fold(qh), jnp.repeat(jnp.arange(b), nqb), jnp.tile(jnp.arange(nqb), b))
    if n_sel > 0:
        kmean = kb.astype(F32).reshape(b, MOBA_HEADS, nb, PPB, PAGE_SIZE, MOBA_DH).sum(-2).sum(-2) / MOBA_BLOCK
        gate = jnp.einsum('bhtd,bhjd->bhtj', qh.astype(F32), kmean)
        fully_past = jnp.arange(nb)[None, :] < (jnp.arange(tp) // MOBA_BLOCK)[:, None]
        gate = jnp.where(fully_past, gate, -jnp.inf)
        gval, sel = lax.top_k(gate, n_sel)
        xs = xs + (fold(sel), fold(gval > -jnp.inf))
    h_idx = jnp.arange(MOBA_HEADS)[:, None, None]

    def body(item):
        q_i, b_i, qb_i = item[0], item[1], item[2]
        qpos = qb_i * MOBA_QB + jnp.arange(MOBA_QB)
        own = (qb_i * MOBA_QB) // MOBA_BLOCK
        k_own = kb[b_i, :, own]
        v_own = vb[b_i, :, own]
        kpos = own * MOBA_BLOCK + jnp.arange(MOBA_BLOCK)
        s_own = jnp.einsum('hqd,hkd->hqk', q_i, k_own).astype(F32) \
            - slopes[:, None, None] * (qpos[:, None] - kpos[None, :]).astype(F32)[None]
        s_own = jnp.where((kpos[None, :] <= qpos[:, None])[None], s_own, -jnp.inf)
        if n_sel == 0:
            p = jax.nn.softmax(s_own, axis=-1)
            return jnp.einsum('hqk,hkd->hqd', p.astype(v_own.dtype), v_own)
        sel_i, valid_i = item[3], item[4]
        k_sel = kb[b_i, h_idx, sel_i].reshape(MOBA_HEADS, MOBA_QB, n_sel * MOBA_BLOCK, MOBA_DH)
        v_sel = vb[b_i, h_idx, sel_i].reshape(MOBA_HEADS, MOBA_QB, n_sel * MOBA_BLOCK, MOBA_DH)
        kpos_sel = (sel_i[..., None] * MOBA_BLOCK + jnp.arange(MOBA_BLOCK)).reshape(MOBA_HEADS, MOBA_QB, -1)
        s_sel = jnp.einsum('hqd,hqkd->hqk', q_i, k_sel).astype(F32) \
            - slopes[:, None, None] * (qpos[None, :, None] - kpos_sel).astype(F32)
        valid_k = jnp.repeat(valid_i, MOBA_BLOCK, axis=-1)
        s_sel = jnp.where(valid_k, s_sel, -jnp.inf)
        p = jax.nn.softmax(jnp.concatenate([s_sel, s_own], axis=-1), axis=-1)
        ns = n_sel * MOBA_BLOCK
        o = jnp.einsum('hqk,hqkd->hqd', p[..., :ns].astype(v_sel.dtype), v_sel) \
            + jnp.einsum('hqk,hkd->hqd', p[..., ns:].astype(v_own.dtype), v_own)
        return o.astype(q_i.dtype)

    o = lax.map(body, xs)
    o = o.reshape(b, nqb, MOBA_HEADS, MOBA_QB, MOBA_DH).transpose(0, 1, 3, 2, 4).reshape(b, tp, MOBA_W)
    return o[:, :t] @ w_out, k_rows, v_rows


def moba_sample(xn, pool_k, pool_v, page_table, w_qkv, w_out):
    b, t, _ = xn.shape
    qkv = (xn @ w_qkv).reshape(b, t, 3, MOBA_HEADS, MOBA_DH)
    q = qkv[:, :, 0] * MOBA_DH ** -0.5
    k_new, v_new = qkv[:, :, 1], qkv[:, :, 2]
    past = page_table.shape[1] * PAGE_SIZE
    n_fp = past // MOBA_BLOCK
    n_sel = min(MOBA_TOPK, n_fp)
    n_tail = past - n_fp * MOBA_BLOCK
    slopes = alibi_slopes(MOBA_HEADS)
    qpos = past + jnp.arange(t)
    kpos_own = n_fp * MOBA_BLOCK + jnp.arange(n_tail + t)
    h_idx = jnp.arange(MOBA_HEADS)[:, None, None, None]

    def one_sequence(item):
        q_s, k_s, v_s, pages = item
        tail = pages[n_fp * PPB:]
        k_own = jnp.concatenate([pool_k[tail].reshape(n_tail, MOBA_HEADS, MOBA_DH).astype(k_s.dtype), k_s], 0)
        v_own = jnp.concatenate([pool_v[tail].reshape(n_tail, MOBA_HEADS, MOBA_DH).astype(v_s.dtype), v_s], 0)
        s_own = jnp.einsum('thd,khd->htk', q_s, k_own).astype(F32) \
            - slopes[:, None, None] * (qpos[:, None] - kpos_own[None, :]).astype(F32)[None]
        s_own = jnp.where((kpos_own[None, :] <= qpos[:, None])[None], s_own, -jnp.inf)
        if n_sel == 0:
            p = jax.nn.softmax(s_own, axis=-1)
            o = jnp.einsum('htk,khd->htd', p.astype(v_own.dtype), v_own)
            return jnp.swapaxes(o, 0, 1).astype(q_s.dtype)
        fp_pages = pages[:n_fp * PPB]
        page_sums = jnp.sum(pool_k[fp_pages].astype(F32), axis=1)
        kmean = page_sums.reshape(n_fp, PPB, MOBA_HEADS, MOBA_DH).sum(1) / MOBA_BLOCK
        gate = jnp.einsum('thd,jhd->htj', q_s.astype(F32), kmean)
        _, sel = lax.top_k(gate, n_sel)
        phys = fp_pages.reshape(n_fp, PPB)[sel]
        k_sel = pool_k[phys, :, h_idx].reshape(MOBA_HEADS, t, n_sel * MOBA_BLOCK, MOBA_DH)
        v_sel = pool_v[phys, :, h_idx].reshape(MOBA_HEADS, t, n_sel * MOBA_BLOCK, MOBA_DH)
        kpos_sel = (sel[..., None] * MOBA_BLOCK + jnp.arange(MOBA_BLOCK)).reshape(MOBA_HEADS, t, -1)
        s_sel = jnp.einsum('thd,htkd->htk', q_s, k_sel.astype(q_s.dtype)).astype(F32) \
            - slopes[:, None, None] * (qpos[None, :, None] - kpos_sel).astype(F32)
        p = jax.nn.softmax(jnp.concatenate([s_sel, s_own], axis=-1), axis=-1)
        ns = n_sel * MOBA_BLOCK
        o = jnp.einsum('htk,htkd->htd', p[..., :ns].astype(v_s.dtype), v_sel.astype(v_s.dtype)) \
            + jnp.einsum('htk,khd->htd', p[..., ns:].astype(v_own.dtype), v_own)
        return jnp.swapaxes(o, 0, 1).astype(q_s.dtype)

    o = lax.map(one_sequence, (q, k_new, v_new, page_table))
    return o.reshape(b, t, MOBA_W) @ w_out, k_new, v_new


def memory_kv(mem, gain, w_kv):
    b, m, _ = mem.shape
    kv = (rmsnorm(mem, gain) @ w_kv).reshape(b, m, 2, XA_HEADS, XA_DH)
    return kv[:, :, 0], kv[:, :, 1]


def memory_attend(xn, mem_k, mem_v, w_q, w_o):
    b, t, _ = xn.shape
    q = (xn @ w_q).reshape(b, t, XA_HEADS, XA_DH)
    s = jnp.einsum('bthd,bmhd->bhtm', q, mem_k.astype(q.dtype)).astype(F32) * XA_DH ** -0.5
    p = jax.nn.softmax(s, axis=-1)
    o = jnp.einsum('bhtm,bmhd->bthd', p.astype(xn.dtype), mem_v.astype(xn.dtype))
    return o.reshape(b, t, XA_W) @ w_o


def conv_ffn(xn, buf, w_up, conv_w, w_down):
    u = xn @ w_up
    c, new_buf = causal_depthwise_conv(u, buf, conv_w)
    hidden = jax.nn.silu(c[..., :D_FF]) * c[..., D_FF:]
    return hidden @ w_down, new_buf


def setup_inputs(seed: int = 0) -> dict:
    key = jax.random.key(seed)
    keys = iter(jax.random.split(key, 48))

    def nrm(shape, scale=1.0):
        return jax.random.normal(next(keys), shape, F32) * scale

    def gain(shape):
        return 1.0 + nrm(shape, 0.01)

    n_pages = PAST_LEN // PAGE_SIZE
    n_used = DEC_BATCH * n_pages
    n_pool = n_used + n_used // 4
    x_prompt = nrm((BATCH, SEQ, D_MODEL))
    x_sample = nrm((DEC_BATCH, DEC_SEQ, D_MODEL))
    state_dn_s = nrm((N_A_LAYERS, DEC_BATCH, DN_HEADS, DN_DK, DN_DV), 0.5)
    state_dn_conv = nrm((N_A_LAYERS, DEC_BATCH, DN_CONV - 1, DN_CONV_CH))
    cache_moba_k = nrm((N_B_LAYERS, n_pool, PAGE_SIZE, MOBA_HEADS, MOBA_DH))
    cache_moba_v = nrm((N_B_LAYERS, n_pool, PAGE_SIZE, MOBA_HEADS, MOBA_DH))
    page_table = jax.random.permutation(next(keys), n_pool)[:n_used].reshape(DEC_BATCH, n_pages).astype(jnp.int32)
    cache_mem_k = nrm((DEPTH, DEC_BATCH, MEM_LEN, XA_HEADS, XA_DH))
    cache_mem_v = nrm((DEPTH, DEC_BATCH, MEM_LEN, XA_HEADS, XA_DH))
    state_ffn_conv = nrm((DEPTH, DEC_BATCH, FFN_CONV - 1, 2 * D_FF))
    mem_prompt = nrm((BATCH, MEM_LEN, D_MODEL))
    dt = jnp.exp(jax.random.uniform(next(keys), (N_A_LAYERS, DN_HEADS), F32, math.log(1e-3), math.log(0.1)))
    return {
        'x_prompt': x_prompt, 'x_sample': x_sample,
        'state_dn_s': state_dn_s, 'state_dn_conv': state_dn_conv,
        'cache_moba_k': cache_moba_k, 'cache_moba_v': cache_moba_v, 'page_table': page_table,
        'cache_mem_k': cache_mem_k, 'cache_mem_v': cache_mem_v, 'state_ffn_conv': state_ffn_conv,
        'mem_prompt': mem_prompt,
        'norm_mix': gain((DEPTH, D_MODEL)), 'norm_xattn': gain((DEPTH, D_MODEL)),
        'norm_mem': gain((DEPTH, D_MODEL)), 'norm_ffn': gain((DEPTH, D_MODEL)), 'norm_final': gain((D_MODEL,)),
        'dn_w_in': nrm((N_A_LAYERS, D_MODEL, DN_PROJ), D_MODEL ** -0.5),
        'dn_conv_w': nrm((N_A_LAYERS, DN_CONV, DN_CONV_CH), DN_CONV ** -0.5),
        'dn_a_log': jnp.log(jax.random.uniform(next(keys), (N_A_LAYERS, DN_HEADS), F32, 1.0, 16.0)),
        'dn_dt_bias': dt + jnp.log(-jnp.expm1(-dt)),
        'dn_o_gain': gain((N_A_LAYERS, DN_DV)),
        'dn_w_out': nrm((N_A_LAYERS, DN_VW, D_MODEL), DN_VW ** -0.5),
        'moba_w_qkv': nrm((N_B_LAYERS, D_MODEL, 3 * MOBA_W), D_MODEL ** -0.5),
        'moba_w_out': nrm((N_B_LAYERS, MOBA_W, D_MODEL), MOBA_W ** -0.5),
        'xa_w_q': nrm((DEPTH, D_MODEL, XA_W), D_MODEL ** -0.5),
        'xa_w_kv': nrm((DEPTH, D_MODEL, 2 * XA_W), D_MODEL ** -0.5),
        'xa_w_out': nrm((DEPTH, XA_W, D_MODEL), XA_W ** -0.5),
        'ffn_w_up': nrm((DEPTH, D_MODEL, 2 * D_FF), D_MODEL ** -0.5),
        'ffn_conv_w': nrm((DEPTH, FFN_CONV, 2 * D_FF), FFN_CONV ** -0.5),
        'ffn_w_down': nrm((DEPTH, D_FF, D_MODEL), D_FF ** -0.5),
    }


def reference(x_prompt, x_sample, state_dn_s, state_dn_conv, cache_moba_k, cache_moba_v, page_table,
              cache_mem_k, cache_mem_v, state_ffn_conv, mem_prompt,
              norm_mix, norm_xattn, norm_mem, norm_ffn, norm_final,
              dn_w_in, dn_conv_w, dn_a_log, dn_dt_bias, dn_o_gain, dn_w_out,
              moba_w_qkv, moba_w_out, xa_w_q, xa_w_kv, xa_w_out,
              ffn_w_up, ffn_conv_w, ffn_w_down):
    hp, hs = x_prompt, x_sample
    bp, bs = hp.shape[0], hs.shape[0]
    p_dn_s, p_dn_c, s_dn_s, s_dn_c = [], [], [], []
    p_mk, p_mv, s_mk, s_mv = [], [], [], []
    p_memk, p_memv, p_ffc, s_ffc = [], [], [], []
    for layer in range(DEPTH):
        xp_n = rmsnorm(hp, norm_mix[layer])
        xs_n = rmsnorm(hs, norm_mix[layer])
        if layer % N_MIXERS == 0:
            ia = layer // N_MIXERS
            w = (dn_w_in[ia], dn_conv_w[ia], dn_a_log[ia], dn_dt_bias[ia], dn_o_gain[ia], dn_w_out[ia])
            s0 = jnp.zeros((bp, DN_HEADS, DN_DK, DN_DV), F32)
            c0 = jnp.zeros((bp, DN_CONV - 1, DN_CONV_CH), hp.dtype)
            out_p, sp, cp = deltanet_mixer(xp_n, s0, c0, *w)
            out_s, ss, cs = deltanet_mixer(xs_n, state_dn_s[ia], state_dn_conv[ia], *w)
            p_dn_s.append(sp); p_dn_c.append(cp); s_dn_s.append(ss); s_dn_c.append(cs)
        else:
            ib = layer // N_MIXERS
            out_p, kp, vp = moba_prompt(xp_n, moba_w_qkv[ib], moba_w_out[ib])
            out_s, ks, vs = moba_sample(xs_n, cache_moba_k[ib], cache_moba_v[ib], page_table,
                                        moba_w_qkv[ib], moba_w_out[ib])
            p_mk.append(kp); p_mv.append(vp); s_mk.append(ks); s_mv.append(vs)
        hp = hp + out_p
        hs = hs + out_s
        mem_k, mem_v = memory_kv(mem_prompt, norm_mem[layer], xa_w_kv[layer])
        p_memk.append(mem_k); p_memv.append(mem_v)
        hp = hp + memory_attend(rmsnorm(hp, norm_xattn[layer]), mem_k, mem_v, xa_w_q[layer], xa_w_out[layer])
        hs = hs + memory_attend(rmsnorm(hs, norm_xattn[layer]), cache_mem_k[layer], cache_mem_v[layer],
                                xa_w_q[layer], xa_w_out[layer])
        f0 = jnp.zeros((bp, FFN_CONV - 1, 2 * D_FF), hp.dtype)
        fp_out, fp_buf = conv_ffn(rmsnorm(hp, norm_ffn[layer]), f0, ffn_w_up[layer], ffn_conv_w[layer], ffn_w_down[layer])
        fs_out, fs_buf = conv_ffn(rmsnorm(hs, norm_ffn[layer]), state_ffn_conv[layer], ffn_w_up[layer],
                                  ffn_conv_w[layer], ffn_w_down[layer])
        hp = hp + fp_out
        hs = hs + fs_out
        p_ffc.append(fp_buf); s_ffc.append(fs_buf)
    y_prompt = rmsnorm(hp, norm_final)
    y_sample = rmsnorm(hs, norm_final)
    return (y_prompt, y_sample,
            jnp.stack(p_dn_s), jnp.stack(p_dn_c), jnp.stack(p_mk), jnp.stack(p_mv),
            jnp.stack(p_memk), jnp.stack(p_memv), jnp.stack(p_ffc),
            jnp.stack(s_dn_s), jnp.stack(s_dn_c), jnp.stack(s_mk), jnp.stack(s_mv), jnp.stack(s_ffc))
```

```python
import functools

import jax
import jax.numpy as jnp
from jax import lax
from jax.experimental import pallas as pl
from jax.experimental.pallas import tpu as pltpu

F32 = jnp.float32
BF16 = jnp.bfloat16
EPS = 1e-6
NEG = -1e30

DN_HEADS = 8
DN_DK = 128
DN_DV = 128
DN_CONV = 4
DN_CHUNK = 64
MOBA_HEADS = 8
MOBA_DH = 128
MOBA_BLOCK = 256
MOBA_TOPK = 3
PAGE_SIZE = 128
XA_HEADS = 4
XA_DH = 128
FFN_CONV = 3

SUBLANES = 8
LANES = 128
VMEM_LIMIT = 56 * 1024 * 1024


def _cparams(sem):
    return pltpu.CompilerParams(dimension_semantics=sem, vmem_limit_bytes=VMEM_LIMIT)


def _resident(shape):
    nd = len(shape)
    return pl.BlockSpec(shape, lambda *_: (0,) * nd, pipeline_mode=pl.Buffered(1))


def _rms(x, gain):
    return x * lax.rsqrt(jnp.mean(x * x, axis=-1, keepdims=True) + EPS) * gain


def _silu(x):
    return x * jax.nn.sigmoid(x)


def _dot(a, b):
    return jnp.dot(a, b, preferred_element_type=F32)


def _dot_nt(a, b):
    return lax.dot_general(a, b, (((1,), (1,)), ((), ())), preferred_element_type=F32)


def _split2(a):
    hi = a.astype(BF16)
    lo = (a - hi.astype(F32)).astype(BF16)
    return hi, lo


def _dot3(a, b, nt=False):
    d = _dot_nt if nt else _dot
    ah, al = _split2(a)
    bh, bl = _split2(b)
    return d(ah, bh) + (d(al, bh) + d(ah, bl))


def _split3(a):
    p1 = a.astype(BF16)
    r1 = a - p1.astype(F32)
    p2 = r1.astype(BF16)
    r2 = r1 - p2.astype(F32)
    return p1, p2, r2.astype(BF16)


def _dot_sel(sel, x, nt=False):
    d = _dot_nt if nt else _dot
    p1, p2, p3 = _split3(x)
    return d(sel, p1) + (d(sel, p2) + d(sel, p3))


def _norm_proj_kernel(x_ref, g_ref, *refs, n_w):
    w_refs, o_refs = refs[:n_w], refs[n_w:]
    xn = _rms(x_ref[...], g_ref[...]).astype(BF16)
    for w_ref, o_ref in zip(w_refs, o_refs):
        n = w_ref.shape[1]
        for c in range(0, n, 512):
            cw = min(512, n - c)
            o_ref[:, c:c + cw] = _dot(xn, w_ref[:, c:c + cw])


def norm_proj(x, gain, ws, tm):
    r, d = x.shape
    tm = min(tm, r)
    assert r % tm == 0
    n_w = len(ws)
    return pl.pallas_call(
        functools.partial(_norm_proj_kernel, n_w=n_w),
        grid=(r // tm,),
        in_specs=[pl.BlockSpec((tm, d), lambda i: (i, 0)), _resident((1, d))]
        + [_resident(w.shape) for w in ws],
        out_specs=[pl.BlockSpec((tm, w.shape[1]), lambda i: (i, 0)) for w in ws],
        out_shape=[jax.ShapeDtypeStruct((r, w.shape[1]), F32) for w in ws],
        compiler_params=_cparams(("parallel",)),
        name="norm_proj",
    )(x, gain.reshape(1, d), *ws)


def _out_proj_kernel(a_ref, w_ref, h_ref, o_ref):
    o_ref[...] = h_ref[...] + _dot(a_ref[...].astype(BF16), w_ref[...])


def out_proj_res(a, w, h, tm):
    r, k = a.shape
    d = w.shape[1]
    tm = min(tm, r)
    assert r % tm == 0
    return pl.pallas_call(
        _out_proj_kernel,
        grid=(r // tm,),
        in_specs=[pl.BlockSpec((tm, k), lambda i: (i, 0)), _resident(w.shape),
                  pl.BlockSpec((tm, d), lambda i: (i, 0))],
        out_specs=pl.BlockSpec((tm, d), lambda i: (i, 0)),
        out_shape=jax.ShapeDtypeStruct((r, d), F32),
        compiler_params=_cparams(("parallel",)),
        name="out_proj_res",
    )(a, w, h)


def _ffn_kernel(x_ref, buf_ref, g_ref, wup_ref, cw_ref, wdn_ref, fg_ref, o_ref, tail_ref, ubuf,
                *, tm, stride, pad, d_ff, final_norm, chunk):
    t = pl.program_id(1)

    @pl.when(t == 0)
    def _():
        ubuf[0:pad, :] = buf_ref[0]

    x = x_ref[0]
    xn = _rms(x, g_ref[...]).astype(BF16)
    o_ref[0] = x
    for j0 in range(0, d_ff, chunk):
        cs = []
        for c0 in (j0, d_ff + j0):
            cols = slice(c0, c0 + chunk)
            u = _dot(xn, wup_ref[:, cols])
            ubuf[pad:pad + tm, cols] = u
            cs.append(ubuf[pad - 2 * stride:pad - 2 * stride + tm, cols] * cw_ref[0:1, cols]
                      + ubuf[pad - stride:pad - stride + tm, cols] * cw_ref[1:2, cols]
                      + u * cw_ref[2:3, cols])
        hid = (_silu(cs[0]) * cs[1]).astype(BF16)
        o_ref[0] += _dot(hid, wdn_ref[j0:j0 + chunk, :])
    if final_norm:
        o_ref[0] = _rms(o_ref[0], fg_ref[...])
    tail = ubuf[tm:tm + pad, :]
    tail_ref[0] = tail
    ubuf[0:pad, :] = tail


def conv_ffn(h, buf, gain, w_up, conv_w, w_down, final_gain, *, tm, stride, final_norm):
    nb, t, d = h.shape
    pad = buf.shape[1]
    d_ff = w_down.shape[0]
    tm = min(tm, t)
    assert t % tm == 0 and pad % SUBLANES == 0 and pad >= 2 * stride and tm >= pad
    kern = functools.partial(_ffn_kernel, tm=tm, stride=stride, pad=pad, d_ff=d_ff,
                             final_norm=final_norm, chunk=256)
    return pl.pallas_call(
        kern,
        grid=(nb, t // tm),
        in_specs=[pl.BlockSpec((1, tm, d), lambda b, i: (b, i, 0)),
                  pl.BlockSpec((1, pad, 2 * d_ff), lambda b, i: (b, 0, 0)),
                  _resident((1, d)), _resident(w_up.shape), _resident(conv_w.shape),
                  _resident(w_down.shape), _resident((1, d))],
        out_specs=[pl.BlockSpec((1, tm, d), lambda b, i: (b, i, 0)),
                   pl.BlockSpec((1, pad, 2 * d_ff), lambda b, i: (b, 0, 0))],
        out_shape=[jax.ShapeDtypeStruct((nb, t, d), F32),
                   jax.ShapeDtypeStruct((nb, pad, 2 * d_ff), F32)],
        scratch_shapes=[pltpu.VMEM((pad + tm, 2 * d_ff), F32)],
        compiler_params=_cparams(("parallel", "arbitrary")),
        name="conv_ffn",
    )(h, buf, gain.reshape(1, d), w_up, conv_w, w_down, final_gain.reshape(1, d))


def _xattn_kernel(x_ref, mk_ref, mv_ref, g_ref, wq_ref, wo_ref, o_ref, att, *, nb, tm):
    x = x_ref[...]
    xn = _rms(x, g_ref[...]).astype(BF16)
    q = _dot(xn, wq_ref[...])
    scale = XA_DH ** -0.5
    for b in range(nb):
        rows = slice(b * tm, (b + 1) * tm)
        for hh in range(XA_HEADS):
            cols = slice(hh * XA_DH, (hh + 1) * XA_DH)
            s = _dot_nt(q[rows, cols].astype(BF16), mk_ref[b, :, cols].astype(BF16)) * scale
            p = jnp.exp(s - jnp.max(s, axis=-1, keepdims=True))
            p = p / jnp.sum(p, axis=-1, keepdims=True)
            att[rows, cols] = _dot(p.astype(BF16), mv_ref[b, :, cols].astype(BF16))
    o_ref[...] = x + _dot(att[...].astype(BF16), wo_ref[...])


def mem_xattn(h, mem_k, mem_v, gain, w_q, w_o, *, nb, tm):
    r, d = h.shape
    n_seq, m, w = mem_k.shape
    t = r // n_seq
    assert t % tm == 0 and (nb == 1 or tm == t) and n_seq % nb == 0
    tiles = t // tm
    kern = functools.partial(_xattn_kernel, nb=nb, tm=tm)
    return pl.pallas_call(
        kern,
        grid=(n_seq // nb, tiles),
        in_specs=[pl.BlockSpec((nb * tm, d), lambda b, i: (b * tiles + i, 0)),
                  pl.BlockSpec((nb, m, w), lambda b, i: (b, 0, 0)),
                  pl.BlockSpec((nb, m, w), lambda b, i: (b, 0, 0)),
                  _resident((1, d)), _resident(w_q.shape), _resident(w_o.shape)],
        out_specs=pl.BlockSpec((nb * tm, d), lambda b, i: (b * tiles + i, 0)),
        out_shape=jax.ShapeDtypeStruct((r, d), F32),
        scratch_shapes=[pltpu.VMEM((nb * tm, w), F32)],
        compiler_params=_cparams(("parallel", "arbitrary")),
        name="mem_xattn",
    )(h, mem_k, mem_v, gain.reshape(1, d), w_q, w_o)


def _tri_inv(a, c):
    r = lax.broadcasted_iota(jnp.int32, (c, c), 0)
    q = lax.broadcasted_iota(jnp.int32, (c, c), 1)
    eye = (r == q).astype(F32)
    blk = min(16, c)
    sh = blk.bit_length() - 1
    d = jnp.where((r >> sh) == (q >> sh), a, 0.0)
    x = eye - d
    p = d
    for _ in range(sh - 1):
        p = _dot3(p, p)
        x = x + _dot3(x, p)
    while blk < c:
        sh += 1
        low = jnp.where(((r >> sh) == (q >> sh)) & ((r >> (sh - 1)) != (q >> (sh - 1))), a, 0.0)
        x = x - _dot3(x, _dot3(low, x))
        blk *= 2
    return x


def _dn_kernel(q_ref, k_ref, v_ref, z_ref, ba_ref, bq_ref, bk_ref, bv_ref, s0_ref,
               cwq_ref, cwk_ref, cwv_ref, avec_ref, dtb_ref, og_ref,
               o_ref, sfin_ref, xq, xk, xv, state, *, tt, c, t_valid, nt):
    h = pl.program_id(1)
    t = pl.program_id(2)
    lead = SUBLANES

    @pl.when(t == 0)
    def _():
        state[...] = s0_ref[0, 0]
        xq[0:lead] = bq_ref[0]
        xk[0:lead] = bk_ref[0]
        xv[0:lead] = bv_ref[0]

    def conv_silu(xb, x_ref, cw_ref):
        xb[lead:lead + tt] = x_ref[0]
        y = xb[lead - 3:lead - 3 + tt] * cw_ref[0:1]
        for j in range(1, DN_CONV):
            y = y + xb[lead - 3 + j:lead - 3 + j + tt] * cw_ref[j:j + 1]
        xb[0:lead] = xb[tt:tt + lead]
        return _silu(y)

    q = conv_silu(xq, q_ref, cwq_ref)
    k = conv_silu(xk, k_ref, cwk_ref)
    v = conv_silu(xv, v_ref, cwv_ref)
    q = q * lax.rsqrt(jnp.sum(q * q, axis=-1, keepdims=True) + EPS) * (DN_DK ** -0.5)
    k = k * lax.rsqrt(jnp.sum(k * k, axis=-1, keepdims=True) + EPS)

    ba = ba_ref[0]
    lane = lax.broadcasted_iota(jnp.int32, (tt, LANES), 1)
    beta_all = jax.nn.sigmoid(ba)
    g_all = -avec_ref[...] * jax.nn.softplus(ba + dtb_ref[...])
    if t_valid < nt * tt:
        row = t * tt + lax.broadcasted_iota(jnp.int32, (tt, LANES), 0)
        beta_all = jnp.where(row < t_valid, beta_all, 0.0)
        g_all = jnp.where(row < t_valid, g_all, 0.0)
    beta = jnp.sum(jnp.where(lane == h, beta_all, 0.0), axis=1, keepdims=True)
    g_sel = jnp.where(lane == h + DN_HEADS, g_all, 0.0)

    ri = lax.broadcasted_iota(jnp.int32, (c, c), 0)
    ci = lax.broadcasted_iota(jnp.int32, (c, c), 1)
    causal = ri >= ci
    strict = ri > ci
    ltri = causal.astype(BF16)
    ones8 = jnp.ones((SUBLANES, LANES), BF16)
    og = og_ref[...]

    for ch in range(tt // c):
        sl = slice(ch * c, (ch + 1) * c)
        qc, kc, vc, bc = q[sl], k[sl], v[sl], beta[sl]
        gc = _dot_sel(ltri, g_sel[sl])
        gcol = jnp.sum(gc, axis=1, keepdims=True)
        grow = _dot_sel(ones8, gc, nt=True)[0:1]
        eg = jnp.exp(gcol)
        kb = kc * bc
        kq = _dot_nt(jnp.concatenate([kb, qc], axis=0).astype(BF16), kc.astype(BF16))
        gam = jnp.exp(jnp.where(causal, gcol - grow, NEG))
        a_low = jnp.where(strict, kq[:c] * gam, 0.0)
        qk = kq[c:] * gam
        t_inv = _tri_inv(a_low, c)
        uw = _dot(t_inv.astype(BF16), jnp.concatenate([vc * bc, kb * eg], axis=1).astype(BF16))
        s = state[...]
        ws_qs = _dot(jnp.concatenate([uw[:, DN_DV:], qc * eg], axis=0).astype(BF16), s.astype(BF16))
        v_new = uw[:, :DN_DV] - ws_qs[:c]
        o = ws_qs[c:] + _dot(qk.astype(BF16), v_new.astype(BF16))
        glast = gcol[c - 1:c]
        kd = kc * jnp.exp(glast - gcol)
        state[...] = s * jnp.exp(glast) + lax.dot_general(
            kd.astype(BF16), v_new.astype(BF16), (((0,), (0,)), ((), ())), preferred_element_type=F32)
        o = o * lax.rsqrt(jnp.mean(o * o, axis=-1, keepdims=True) + EPS) * og
        o_ref[0, sl, :] = o * _silu(z_ref[0, sl, :])

    @pl.when(t == nt - 1)
    def _():
        sfin_ref[0, 0] = state[...]


def deltanet_core(qkv, z, ba, buf, s0, conv_w, a_log, dt_bias, o_gain, *, tt, c, t_valid):
    b, tp, _ = qkv.shape
    hh = DN_HEADS
    nt = tp // tt
    assert tp % tt == 0 and tt % c == 0
    zpad = jnp.zeros((LANES - 2 * hh,), F32)
    avec = jnp.concatenate([jnp.zeros((hh,), F32), jnp.exp(a_log), zpad]).reshape(1, LANES)
    dtb = jnp.concatenate([jnp.zeros((hh,), F32), dt_bias, zpad]).reshape(1, LANES)
    col = lambda off: pl.BlockSpec((1, tt, DN_DK), lambda i, h, t: (i, t, off + h))
    bufc = lambda off: pl.BlockSpec((1, SUBLANES, DN_DK), lambda i, h, t: (i, 0, off + h))
    cwc = lambda off: pl.BlockSpec((DN_CONV, DN_DK), lambda i, h, t: (0, off + h))
    kern = functools.partial(_dn_kernel, tt=tt, c=c, t_valid=t_valid, nt=nt)
    return pl.pallas_call(
        kern,
        grid=(b, hh, nt),
        in_specs=[col(0), col(hh), col(2 * hh),
                  pl.BlockSpec((1, tt, DN_DV), lambda i, h, t: (i, t, h)),
                  pl.BlockSpec((1, tt, LANES), lambda i, h, t: (i, t, 0)),
                  bufc(0), bufc(hh), bufc(2 * hh),
                  pl.BlockSpec((1, 1, DN_DK, DN_DV), lambda i, h, t: (i, h, 0, 0)),
                  cwc(0), cwc(hh), cwc(2 * hh),
                  pl.BlockSpec((1, LANES), lambda i, h, t: (0, 0)),
                  pl.BlockSpec((1, LANES), lambda i, h, t: (0, 0)),
                  pl.BlockSpec((1, DN_DV), lambda i, h, t: (0, 0))],
        out_specs=[pl.BlockSpec((1, tt, DN_DV), lambda i, h, t: (i, t, h)),
                   pl.BlockSpec((1, 1, DN_DK, DN_DV), lambda i, h, t: (i, h, 0, 0))],
        out_shape=[jax.ShapeDtypeStruct((b, tp, hh * DN_DV), F32),
                   jax.ShapeDtypeStruct((b, hh, DN_DK, DN_DV), F32)],
        scratch_shapes=[pltpu.VMEM((SUBLANES + tt, DN_DK), F32), pltpu.VMEM((SUBLANES + tt, DN_DK), F32),
                        pltpu.VMEM((SUBLANES + tt, DN_DV), F32), pltpu.VMEM((DN_DK, DN_DV), F32)],
        compiler_params=_cparams(("parallel", "parallel", "arbitrary")),
        name="deltanet_core",
    )(qkv, qkv, qkv, z, ba, buf, buf, buf, s0, conv_w, conv_w, conv_w, avec, dtb, o_gain.reshape(1, DN_DV))


def _topk_mask_rows(g, n_valid, n_rows, k_top):
    jrow = lax.broadcasted_iota(jnp.int32, g.shape, 0)
    cnt = jnp.zeros(g.shape, F32)
    for jp in range(n_rows):
        gb = g[jp:jp + 1, :]
        beats = jnp.where(gb > g, 1.0, jnp.where(gb == g, jnp.where(jrow > jp, 1.0, 0.0), 0.0))
        cnt = cnt + jnp.where(jp < n_valid, beats, 0.0)
    return jnp.where(jrow < n_valid, cnt, float(k_top)) < float(k_top), cnt


def _moba_kernel(q_ref, k_ref, v_ref, sl_ref, o_ref, kmean, vt, pen, *, nb, nbp):
    i = pl.program_id(2)
    blk = MOBA_BLOCK

    @pl.when(i == 0)
    def _():
        kmean[...] = jnp.zeros_like(kmean)
        for j in range(nb):
            rows = slice(j * blk, (j + 1) * blk)
            kmean[j:j + 1, :] = jnp.sum(k_ref[0, rows, :], axis=0, keepdims=True) * (1.0 / blk)
            vt[j] = v_ref[0, rows, :].T.astype(BF16)

    slope = sl_ref[0]
    qt = (q_ref[0] * (MOBA_DH ** -0.5)).T
    qtb = qt.astype(BF16)
    gate = _dot3(kmean[...], qt)
    sel, _ = _topk_mask_rows(gate, i, nb, MOBA_TOPK)
    jrow = lax.broadcasted_iota(jnp.int32, (nbp, blk), 0)
    pen[...] = jnp.where(sel, 0.0, NEG) - slope * ((i - jrow) * blk).astype(F32)

    key = lax.broadcasted_iota(jnp.int32, (blk, blk), 0)
    qry = lax.broadcasted_iota(jnp.int32, (blk, blk), 1)
    rel = slope * (key - qry).astype(F32)

    own = pl.multiple_of(i * blk, blk)
    s = _dot(k_ref[0, pl.ds(own, blk), :].astype(BF16), qtb) + rel
    s = jnp.where(key <= qry, s, NEG)
    m = jnp.max(s, axis=0, keepdims=True)
    p = jnp.exp(s - m)
    l = jnp.sum(p, axis=0, keepdims=True)
    acc = _dot(vt[i], p.astype(BF16))

    def body(j, carry):
        m, l, acc = carry
        kj = k_ref[0, pl.ds(pl.multiple_of(j * blk, blk), blk), :].astype(BF16)
        s = _dot(kj, qtb) + rel + pen[pl.ds(j, 1), :]
        m_new = jnp.maximum(m, jnp.max(s, axis=0, keepdims=True))
        alpha = jnp.exp(m - m_new)
        p = jnp.exp(s - m_new)
        l = alpha * l + jnp.sum(p, axis=0, keepdims=True)
        acc = alpha * acc + _dot(vt[j], p.astype(BF16))
        return m_new, l, acc

    m, l, acc = lax.fori_loop(0, i, body, (m, l, acc))
    o_ref[0] = (acc / l).T


def _alibi_rows(n_heads, width):
    slopes = jnp.exp2(-8.0 * jnp.arange(1, n_heads + 1, dtype=F32) / n_heads)
    return jnp.broadcast_to(slopes[:, None, None], (n_heads, 1, width))


def moba_prompt_attn(q, k, v):
    b, t, w = q.shape
    hh, dh, blk = MOBA_HEADS, MOBA_DH, MOBA_BLOCK
    assert t % blk == 0
    nb = t // blk
    nbp = -(-nb // SUBLANES) * SUBLANES
    kern = functools.partial(_moba_kernel, nb=nb, nbp=nbp)
    return pl.pallas_call(
        kern,
        grid=(b, hh, nb),
        in_specs=[pl.BlockSpec((1, blk, dh), lambda bi, h, i: (bi, i, h)),
                  pl.BlockSpec((1, t, dh), lambda bi, h, i: (bi, 0, h)),
                  pl.BlockSpec((1, t, dh), lambda bi, h, i: (bi, 0, h)),
                  pl.BlockSpec((1, 1, blk), lambda bi, h, i: (h, 0, 0))],
        out_specs=pl.BlockSpec((1, blk, dh), lambda bi, h, i: (bi, i, h)),
        out_shape=jax.ShapeDtypeStruct((b, t, w), F32),
        scratch_shapes=[pltpu.VMEM((nbp, dh), F32), pltpu.VMEM((nb, dh, blk), BF16),
                        pltpu.VMEM((nbp, blk), F32)],
        compiler_params=_cparams(("parallel", "parallel", "arbitrary")),
        name="moba_prompt_attn",
    )(q, k, v, _alibi_rows(hh, blk))


PAGES_PER_STEP = 8
PPB = MOBA_BLOCK // PAGE_SIZE


def _kmean_kernel(pt_ref, *refs):
    page_refs, o_ref = refs[:PAGES_PER_STEP], refs[PAGES_PER_STEP]
    for u in range(0, PAGES_PER_STEP, PPB):
        tot = jnp.sum(page_refs[u][...], axis=0)
        for w in range(1, PPB):
            tot = tot + jnp.sum(page_refs[u + w][...], axis=0)
        o_ref[0, u // PPB] = tot * (1.0 / MOBA_BLOCK)


def moba_block_means(pool_k, layer, page_table):
    n_seq, n_pages = page_table.shape
    _, _, page, hh, dh = pool_k.shape
    assert n_pages % PAGES_PER_STEP == 0 and PAGES_PER_STEP % PPB == 0
    steps = n_pages // PAGES_PER_STEP

    def page_spec(u):
        return pl.BlockSpec((None, None, page, hh, dh),
                            lambda b, g, pt: (layer, pt[b * n_pages + g * PAGES_PER_STEP + u], 0, 0, 0))

    bps = PAGES_PER_STEP // PPB
    return pl.pallas_call(
        _kmean_kernel,
        grid_spec=pltpu.PrefetchScalarGridSpec(
            num_scalar_prefetch=1, grid=(n_seq, steps),
            in_specs=[page_spec(u) for u in range(PAGES_PER_STEP)],
            out_specs=pl.BlockSpec((1, bps, hh, dh), lambda b, g, pt: (b, g, 0, 0))),
        out_shape=jax.ShapeDtypeStruct((n_seq, n_pages // PPB, hh, dh), F32),
        compiler_params=_cparams(("parallel", "arbitrary")),
        name="moba_block_means",
    )(page_table.reshape(-1), *([pool_k] * PAGES_PER_STEP))


def _moba_pick_kernel(q_ref, km_ref, o_ref, *, n_blk, t):
    hh, dh = MOBA_HEADS, MOBA_DH
    q = q_ref[0] * (dh ** -0.5)
    cols = []
    for h in range(hh):
        sl = slice(h * dh, (h + 1) * dh)
        cols.append(_dot3(km_ref[0, :, sl], q[:, sl], nt=True))
    gate = jnp.concatenate(cols, axis=1)
    _, cnt = _topk_mask_rows(gate, n_blk, n_blk, MOBA_TOPK)
    jrow = lax.broadcasted_iota(jnp.int32, gate.shape, 0).astype(F32)
    rows = [jnp.sum(jnp.where(cnt == float(r), jrow, 0.0), axis=0, keepdims=True)
            for r in range(MOBA_TOPK)]
    rows.append(jnp.zeros((SUBLANES - MOBA_TOPK, hh * t), F32))
    o_ref[0] = jnp.concatenate(rows, axis=0).astype(jnp.int32)


def moba_pick_blocks(q, kmean):
    n_seq, t, w = q.shape
    n_blk = kmean.shape[1]
    kern = functools.partial(_moba_pick_kernel, n_blk=n_blk, t=t)
    return pl.pallas_call(
        kern,
        grid=(n_seq,),
        in_specs=[pl.BlockSpec((1, t, w), lambda b: (b, 0, 0)),
                  pl.BlockSpec((1, n_blk, w), lambda b: (b, 0, 0))],
        out_specs=pl.BlockSpec((1, SUBLANES, MOBA_HEADS * t), lambda b: (b, 0, 0)),
        out_shape=jax.ShapeDtypeStruct((n_seq, SUBLANES, MOBA_HEADS * t), jnp.int32),
        compiler_params=_cparams(("parallel",)),
        name="moba_pick_blocks",
    )(q, kmean)


def _moba_dec_kernel(pt_ref, sel_ref, q_ref, kn_ref, vn_ref, sl_ref, pk_hbm, pv_hbm, o_ref,
                     kbuf, vbuf, sem, *, layer, t_len, past, n_pages):
    n_pg = MOBA_TOPK * PPB
    b, h = pl.program_id(0), pl.program_id(1)
    page = PAGE_SIZE

    def blk_of(t, r):
        return sel_ref[((b * MOBA_TOPK + r) * MOBA_HEADS + h) * t_len + t]

    def copies(t):
        out = []
        for r in range(MOBA_TOPK):
            for u in range(PPB):
                phys = pt_ref[b * n_pages + blk_of(t, r) * PPB + u]
                i = r * PPB + u
                out.append(pltpu.make_async_copy(pk_hbm.at[layer, phys, :, h, :], kbuf.at[t, i], sem.at[0, t]))
                out.append(pltpu.make_async_copy(pv_hbm.at[layer, phys, :, h, :], vbuf.at[t, i], sem.at[1, t]))
        return out

    for t in range(t_len):
        for cp in copies(t):
            cp.start()

    slope = sl_ref[0][:, :page]
    q = (q_ref[0] * (MOBA_DH ** -0.5)).astype(BF16)
    qpos = past + lax.broadcasted_iota(jnp.int32, (t_len, page), 0)
    lane = lax.broadcasted_iota(jnp.int32, (t_len, page), 1)
    s_own = _dot_nt(q, kn_ref[0].astype(BF16))
    qi = lax.broadcasted_iota(jnp.int32, (t_len, t_len), 0)
    ki = lax.broadcasted_iota(jnp.int32, (t_len, t_len), 1)
    s_own = jnp.where(ki <= qi, s_own - slope[:, :t_len] * (qi - ki).astype(F32), NEG)
    vn = vn_ref[0].astype(BF16)

    for t in range(t_len):
        for cp in copies(t):
            cp.wait()
        scores = [s_own]
        for r in range(MOBA_TOPK):
            for u in range(PPB):
                kpos = blk_of(t, r) * MOBA_BLOCK + u * page + lane
                s = _dot_nt(q, kbuf[t, r * PPB + u].astype(BF16))
                scores.append(s - slope * (qpos - kpos).astype(F32))
        m = scores[0].max(axis=-1, keepdims=True)
        for s in scores[1:]:
            m = jnp.maximum(m, s.max(axis=-1, keepdims=True))
        ps = [jnp.exp(s - m) for s in scores]
        l = ps[0].sum(axis=-1, keepdims=True)
        for p in ps[1:]:
            l = l + p.sum(axis=-1, keepdims=True)
        acc = _dot((ps[0] / l).astype(BF16), vn)
        for i in range(n_pg):
            acc = acc + _dot((ps[1 + i] / l).astype(BF16), vbuf[t, i].astype(BF16))
        o_ref[0, 0, t:t + 1, :] = acc[t:t + 1]


def moba_decode_attn(q, k_new, v_new, pool_k, pool_v, layer, page_table, sel):
    n_seq, t_len, w = q.shape
    n_pages = page_table.shape[1]
    hh, dh = MOBA_HEADS, MOBA_DH
    _, _, page, _, _ = pool_k.shape
    past = n_pages * page
    assert past % MOBA_BLOCK == 0
    n_pg = MOBA_TOPK * PPB
    row = lambda: pl.BlockSpec((1, t_len, dh), lambda b, h, pt, sl: (b, 0, h))
    kern = functools.partial(_moba_dec_kernel, layer=layer, t_len=t_len, past=past, n_pages=n_pages)
    return pl.pallas_call(
        kern,
        grid_spec=pltpu.PrefetchScalarGridSpec(
            num_scalar_prefetch=2, grid=(n_seq, hh),
            in_specs=[row(), row(), row(),
                      pl.BlockSpec((1, 1, MOBA_BLOCK), lambda b, h, pt, sl: (h, 0, 0)),
                      pl.BlockSpec(memory_space=pl.ANY), pl.BlockSpec(memory_space=pl.ANY)],
            out_specs=pl.BlockSpec((1, 1, t_len, dh), lambda b, h, pt, sl: (b, h, 0, 0)),
            scratch_shapes=[pltpu.VMEM((t_len, n_pg, page, dh), F32), pltpu.VMEM((t_len, n_pg, page, dh), F32),
                            pltpu.SemaphoreType.DMA((2, t_len))]),
        out_shape=jax.ShapeDtypeStruct((n_seq, hh, t_len, dh), F32),
        compiler_params=_cparams(("parallel", "arbitrary")),
        name="moba_decode_attn",
    )(page_table.reshape(-1), sel.reshape(-1), q, k_new, v_new, _alibi_rows(hh, MOBA_BLOCK), pool_k, pool_v)


def _last_rows(buf, x, n):
    t = x.shape[1]
    if t >= n:
        return x[:, t - n:]
    return jnp.concatenate([buf[:, buf.shape[1] - (n - t):], x], axis=1)


def _pad_front(buf, rows):
    return jnp.pad(buf, ((0, 0), (rows - buf.shape[1], 0), (0, 0)))


def kernel(x_prompt, x_sample, state_dn_s, state_dn_conv, cache_moba_k, cache_moba_v, page_table, cache_mem_k, cache_mem_v, state_ffn_conv, mem_prompt, norm_mix, norm_xattn, norm_mem, norm_ffn, norm_final, dn_w_in, dn_conv_w, dn_a_log, dn_dt_bias, dn_o_gain, dn_w_out, moba_w_qkv, moba_w_out, xa_w_q, xa_w_kv, xa_w_out, ffn_w_up, ffn_conv_w, ffn_w_down):
    bp, seq, d = x_prompt.shape
    bs, dseq, _ = x_sample.shape
    depth = norm_mix.shape[0]
    mem_len = mem_prompt.shape[1]
    d_ff = ffn_w_down.shape[1]
    qk_w = DN_HEADS * DN_DK
    v_w = DN_HEADS * DN_DV
    conv_ch = 2 * qk_w + v_w
    moba_w = MOBA_HEADS * MOBA_DH
    xa_w = XA_HEADS * XA_DH
    bf = lambda a: a.astype(BF16)

    hp = x_prompt.reshape(bp * seq, d)
    hs = x_sample.reshape(bs * dseq, d)
    outs = {k: [] for k in ("p_dn_s", "p_dn_c", "p_mk", "p_mv", "p_memk", "p_memv", "p_ffc",
                            "s_dn_s", "s_dn_c", "s_mk", "s_mv", "s_ffc")}
    for layer in range(depth):
        if layer % 2 == 0:
            ia = layer // 2
            w_in = dn_w_in[ia]
            ws = [bf(w_in[:, :conv_ch]), bf(w_in[:, conv_ch:conv_ch + v_w]),
                  bf(jnp.pad(w_in[:, conv_ch + v_w:], ((0, 0), (0, LANES - 2 * DN_HEADS))))]
            w_out = bf(dn_w_out[ia])
            args = (dn_conv_w[ia], dn_a_log[ia], dn_dt_bias[ia], dn_o_gain[ia])
            qkv, z, ba = norm_proj(hp, norm_mix[layer], ws, tm=256)
            qkv3 = qkv.reshape(bp, seq, conv_ch)
            o, s_fin = deltanet_core(qkv3, z.reshape(bp, seq, v_w), ba.reshape(bp, seq, LANES),
                                     jnp.zeros((bp, SUBLANES, conv_ch), F32),
                                     jnp.zeros((bp, DN_HEADS, DN_DK, DN_DV), F32), *args,
                                     tt=256, c=DN_CHUNK, t_valid=seq)
            hp = out_proj_res(o.reshape(bp * seq, v_w), w_out, hp, tm=512)
            outs["p_dn_s"].append(s_fin)
            outs["p_dn_c"].append(_last_rows(jnp.zeros((bp, DN_CONV - 1, conv_ch), F32), qkv3, DN_CONV - 1))
            qkv, z, ba = norm_proj(hs, norm_mix[layer], ws, tm=256)
            qkv3 = qkv.reshape(bs, dseq, conv_ch)
            tpad = 16
            padt = lambda a: jnp.pad(a.reshape(bs, dseq, -1), ((0, 0), (0, tpad - dseq), (0, 0)))
            o, s_fin = deltanet_core(padt(qkv), padt(z), padt(ba),
                                     _pad_front(state_dn_conv[ia], SUBLANES), state_dn_s[ia], *args,
                                     tt=tpad, c=tpad, t_valid=dseq)
            hs = out_proj_res(o[:, :dseq].reshape(bs * dseq, v_w), w_out, hs, tm=256)
            outs["s_dn_s"].append(s_fin)
            outs["s_dn_c"].append(_last_rows(state_dn_conv[ia], qkv3, DN_CONV - 1))
        else:
            ib = layer // 2
            w_qkv = moba_w_qkv[ib]
            ws = [bf(w_qkv[:, j * moba_w:(j + 1) * moba_w]) for j in range(3)]
            w_out = bf(moba_w_out[ib])
            q, k, v = (a.reshape(bp, seq, moba_w) for a in norm_proj(hp, norm_mix[layer], ws, tm=256))
            o = moba_prompt_attn(q, k, v)
            hp = out_proj_res(o.reshape(bp * seq, moba_w), w_out, hp, tm=512)
            outs["p_mk"].append(k.reshape(bp, seq, MOBA_HEADS, MOBA_DH))
            outs["p_mv"].append(v.reshape(bp, seq, MOBA_HEADS, MOBA_DH))
            q, k, v = (a.reshape(bs, dseq, moba_w) for a in norm_proj(hs, norm_mix[layer], ws, tm=256))
            kmean = moba_block_means(cache_moba_k, ib, page_table)
            pick = moba_pick_blocks(q, kmean.reshape(bs, -1, moba_w))
            sel = pick[:, :MOBA_TOPK].reshape(bs, MOBA_TOPK, MOBA_HEADS, dseq)
            o = moba_decode_attn(q, k, v, cache_moba_k, cache_moba_v, ib, page_table, sel)
            o = o.transpose(0, 2, 1, 3).reshape(bs * dseq, moba_w)
            hs = out_proj_res(o, w_out, hs, tm=256)
            outs["s_mk"].append(k.reshape(bs, dseq, MOBA_HEADS, MOBA_DH))
            outs["s_mv"].append(v.reshape(bs, dseq, MOBA_HEADS, MOBA_DH))

        w_kv = xa_w_kv[layer]
        mk, mv = norm_proj(mem_prompt.reshape(bp * mem_len, d), norm_mem[layer],
                           [bf(w_kv[:, :xa_w]), bf(w_kv[:, xa_w:])], tm=256)
        mk = mk.reshape(bp, mem_len, xa_w)
        mv = mv.reshape(bp, mem_len, xa_w)
        outs["p_memk"].append(mk.reshape(bp, mem_len, XA_HEADS, XA_DH))
        outs["p_memv"].append(mv.reshape(bp, mem_len, XA_HEADS, XA_DH))
        w_q, w_o = bf(xa_w_q[layer]), bf(xa_w_out[layer])
        hp = mem_xattn(hp, mk, mv, norm_xattn[layer], w_q, w_o, nb=1, tm=512)
        hs = mem_xattn(hs, cache_mem_k[layer].reshape(bs, mem_len, xa_w),
                       cache_mem_v[layer].reshape(bs, mem_len, xa_w),
                       norm_xattn[layer], w_q, w_o, nb=8, tm=dseq)

        last = layer == depth - 1
        w_up, w_dn = bf(ffn_w_up[layer]), bf(ffn_w_down[layer])
        hp3, tail = conv_ffn(hp.reshape(bp, seq, d), jnp.zeros((bp, SUBLANES, 2 * d_ff), F32),
                             norm_ffn[layer], w_up, ffn_conv_w[layer], w_dn, norm_final,
                             tm=256, stride=1, final_norm=last)
        hp = hp3.reshape(bp * seq, d)
        outs["p_ffc"].append(tail[:, SUBLANES - (FFN_CONV - 1):])
        hs_tm = hs.reshape(bs, dseq, d).transpose(1, 0, 2).reshape(1, dseq * bs, d)
        buf_tm = state_ffn_conv[layer].transpose(1, 0, 2).reshape(1, (FFN_CONV - 1) * bs, 2 * d_ff)
        hs3, tail = conv_ffn(hs_tm, buf_tm, norm_ffn[layer], w_up, ffn_conv_w[layer], w_dn, norm_final,
                             tm=dseq * bs, stride=bs, final_norm=last)
        hs = hs3.reshape(dseq, bs, d).transpose(1, 0, 2).reshape(bs * dseq, d)
        outs["s_ffc"].append(tail.reshape(FFN_CONV - 1, bs, 2 * d_ff).transpose(1, 0, 2))

    y_prompt = hp.reshape(bp, seq, d)
    y_sample = hs.reshape(bs, dseq, d)
    st = lambda k: outs[k][0][None] if len(outs[k]) == 1 else jnp.stack(outs[k])
    return (y_prompt, y_sample, st("p_dn_s"), st("p_dn_c"), st("p_mk"), st("p_mv"),
            st("p_memk"), st("p_memv"), st("p_ffc"),
            st("s_dn_s"), st("s_dn_c"), st("s_mk"), st("s_mv"), st("s_ffc"))
```

```python
import functools

import jax
import jax.numpy as jnp
from jax import lax
from jax.experimental import pallas as pl
from jax.experimental.pallas import tpu as pltpu

F32 = jnp.float32
BF16 = jnp.bfloat16
EPS = 1e-6
NEG = -1e30

DN_HEADS = 8
DN_DK = 128
DN_DV = 128
DN_CONV = 4
DN_CHUNK = 64
MOBA_HEADS = 8
MOBA_DH = 128
MOBA_BLOCK = 256
MOBA_TOPK = 3
PAGE_SIZE = 128
XA_HEADS = 4
XA_DH = 128
FFN_CONV = 3

SUBLANES = 8
LANES = 128
VMEM_LIMIT = 56 * 1024 * 1024


def _cparams(sem):
    return pltpu.CompilerParams(dimension_semantics=sem, vmem_limit_bytes=VMEM_LIMIT)


def _resident(shape):
    nd = len(shape)
    return pl.BlockSpec(shape, lambda *_: (0,) * nd, pipeline_mode=pl.Buffered(1))


def _rms(x, gain):
    return x * lax.rsqrt(jnp.mean(x * x, axis=-1, keepdims=True) + EPS) * gain


def _silu(x):
    return x * jax.nn.sigmoid(x)


def _dot(a, b):
    return jnp.dot(a, b, preferred_element_type=F32)


def _dot_nt(a, b):
    return lax.dot_general(a, b, (((1,), (1,)), ((), ())), preferred_element_type=F32)


def _split2(a):
    hi = a.astype(BF16)
    lo = (a - hi.astype(F32)).astype(BF16)
    return hi, lo


def _dot3(a, b, nt=False):
    d = _dot_nt if nt else _dot
    ah, al = _split2(a)
    bh, bl = _split2(b)
    return d(ah, bh) + (d(al, bh) + d(ah, bl))


def _split3(a):
    p1 = a.astype(BF16)
    r1 = a - p1.astype(F32)
    p2 = r1.astype(BF16)
    r2 = r1 - p2.astype(F32)
    return p1, p2, r2.astype(BF16)


def _dot_sel(sel, x, nt=False):
    d = _dot_nt if nt else _dot
    p1, p2, p3 = _split3(x)
    return d(sel, p1) + (d(sel, p2) + d(sel, p3))


def _norm_proj_kernel(x_ref, g_ref, *refs, n_w):
    w_refs, o_refs = refs[:n_w], refs[n_w:]
    xn = _rms(x_ref[...], g_ref[...]).astype(BF16)
    for w_ref, o_ref in zip(w_refs, o_refs):
        n = w_ref.shape[1]
        for c in range(0, n, 512):
            cw = min(512, n - c)
            o_ref[:, c:c + cw] = _dot(xn, w_ref[:, c:c + cw])


def norm_proj(x, gain, ws, tm):
    r, d = x.shape
    tm = min(tm, r)
    assert r % tm == 0
    n_w = len(ws)
    return pl.pallas_call(
        functools.partial(_norm_proj_kernel, n_w=n_w),
        grid=(r // tm,),
        in_specs=[pl.BlockSpec((tm, d), lambda i: (i, 0)), _resident((1, d))]
        + [_resident(w.shape) for w in ws],
        out_specs=[pl.BlockSpec((tm, w.shape[1]), lambda i: (i, 0)) for w in ws],
        out_shape=[jax.ShapeDtypeStruct((r, w.shape[1]), F32) for w in ws],
        compiler_params=_cparams(("parallel",)),
        name="norm_proj",
    )(x, gain.reshape(1, d), *ws)


def _out_proj_kernel(a_ref, w_ref, h_ref, o_ref):
    o_ref[...] = h_ref[...] + _dot(a_ref[...].astype(BF16), w_ref[...])


def out_proj_res(a, w, h, tm):
    r, k = a.shape
    d = w.shape[1]
    tm = min(tm, r)
    assert r % tm == 0
    return pl.pallas_call(
        _out_proj_kernel,
        grid=(r // tm,),
        in_specs=[pl.BlockSpec((tm, k), lambda i: (i, 0)), _resident(w.shape),
                  pl.BlockSpec((tm, d), lambda i: (i, 0))],
        out_specs=pl.BlockSpec((tm, d), lambda i: (i, 0)),
        out_shape=jax.ShapeDtypeStruct((r, d), F32),
        compiler_params=_cparams(("parallel",)),
        name="out_proj_res",
    )(a, w, h)


def _ffn_kernel(x_ref, buf_ref, g_ref, wup_ref, cw_ref, wdn_ref, fg_ref, o_ref, tail_ref, ubuf,
                *, tm, stride, pad, d_ff, final_norm, chunk):
    t = pl.program_id(1)

    @pl.when(t == 0)
    def _():
        ubuf[0:pad, :] = buf_ref[0]

    x = x_ref[0]
    xn = _rms(x, g_ref[...]).astype(BF16)
    o_ref[0] = x
    for j0 in range(0, d_ff, chunk):
        cs = []
        for c0 in (j0, d_ff + j0):
            cols = slice(c0, c0 + chunk)
            u = _dot(xn, wup_ref[:, cols])
            ubuf[pad:pad + tm, cols] = u
            cs.append(ubuf[pad - 2 * stride:pad - 2 * stride + tm, cols] * cw_ref[0:1, cols]
                      + ubuf[pad - stride:pad - stride + tm, cols] * cw_ref[1:2, cols]
                      + u * cw_ref[2:3, cols])
        hid = (_silu(cs[0]) * cs[1]).astype(BF16)
        o_ref[0] += _dot(hid, wdn_ref[j0:j0 + chunk, :])
    if final_norm:
        o_ref[0] = _rms(o_ref[0], fg_ref[...])
    tail = ubuf[tm:tm + pad, :]
    tail_ref[0] = tail
    ubuf[0:pad, :] = tail


def conv_ffn(h, buf, gain, w_up, conv_w, w_down, final_gain, *, tm, stride, final_norm):
    nb, t, d = h.shape
    pad = buf.shape[1]
    d_ff = w_down.shape[0]
    tm = min(tm, t)
    assert t % tm == 0 and pad % SUBLANES == 0 and pad >= 2 * stride and tm >= pad
    kern = functools.partial(_ffn_kernel, tm=tm, stride=stride, pad=pad, d_ff=d_ff,
                             final_norm=final_norm, chunk=256)
    return pl.pallas_call(
        kern,
        grid=(nb, t // tm),
        in_specs=[pl.BlockSpec((1, tm, d), lambda b, i: (b, i, 0)),
                  pl.BlockSpec((1, pad, 2 * d_ff), lambda b, i: (b, 0, 0)),
                  _resident((1, d)), _resident(w_up.shape), _resident(conv_w.shape),
                  _resident(w_down.shape), _resident((1, d))],
        out_specs=[pl.BlockSpec((1, tm, d), lambda b, i: (b, i, 0)),
                   pl.BlockSpec((1, pad, 2 * d_ff), lambda b, i: (b, 0, 0))],
        out_shape=[jax.ShapeDtypeStruct((nb, t, d), F32),
                   jax.ShapeDtypeStruct((nb, pad, 2 * d_ff), F32)],
        scratch_shapes=[pltpu.VMEM((pad + tm, 2 * d_ff), F32)],
        compiler_params=_cparams(("parallel", "arbitrary")),
        name="conv_ffn",
    )(h, buf, gain.reshape(1, d), w_up, conv_w, w_down, final_gain.reshape(1, d))


def _xattn_kernel(x_ref, mk_ref, mv_ref, g_ref, wq_ref, wo_ref, o_ref, att, *, nb, tm):
    x = x_ref[...]
    xn = _rms(x, g_ref[...]).astype(BF16)
    q = _dot(xn, wq_ref[...])
    scale = XA_DH ** -0.5
    for b in range(nb):
        rows = slice(b * tm, (b + 1) * tm)
        for hh in range(XA_HEADS):
            cols = slice(hh * XA_DH, (hh + 1) * XA_DH)
            s = _dot_nt(q[rows, cols].astype(BF16), mk_ref[b, :, cols].astype(BF16)) * scale
            p = jnp.exp(s - jnp.max(s, axis=-1, keepdims=True))
            p = p / jnp.sum(p, axis=-1, keepdims=True)
            att[rows, cols] = _dot(p.astype(BF16), mv_ref[b, :, cols].astype(BF16))
    o_ref[...] = x + _dot(att[...].astype(BF16), wo_ref[...])


def mem_xattn(h, mem_k, mem_v, gain, w_q, w_o, *, nb, tm):
    r, d = h.shape
    n_seq, m, w = mem_k.shape
    t = r // n_seq
    assert t % tm == 0 and (nb == 1 or tm == t) and n_seq % nb == 0
    tiles = t // tm
    kern = functools.partial(_xattn_kernel, nb=nb, tm=tm)
    return pl.pallas_call(
        kern,
        grid=(n_seq // nb, tiles),
        in_specs=[pl.BlockSpec((nb * tm, d), lambda b, i: (b * tiles + i, 0)),
                  pl.BlockSpec((nb, m, w), lambda b, i: (b, 0, 0)),
                  pl.BlockSpec((nb, m, w), lambda b, i: (b, 0, 0)),
                  _resident((1, d)), _resident(w_q.shape), _resident(w_o.shape)],
        out_specs=pl.BlockSpec((nb * tm, d), lambda b, i: (b * tiles + i, 0)),
        out_shape=jax.ShapeDtypeStruct((r, d), F32),
        scratch_shapes=[pltpu.VMEM((nb * tm, w), F32)],
        compiler_params=_cparams(("parallel", "arbitrary")),
        name="mem_xattn",
    )(h, mem_k, mem_v, gain.reshape(1, d), w_q, w_o)


def _tri_inv(a, c):
    r = lax.broadcasted_iota(jnp.int32, (c, c), 0)
    q = lax.broadcasted_iota(jnp.int32, (c, c), 1)
    eye = (r == q).astype(F32)
    blk = min(16, c)
    sh = blk.bit_length() - 1
    n = len(a)
    d = [jnp.where((r >> sh) == (q >> sh), ai, 0.0) for ai in a]
    x = [eye - di for di in d]
    p = d
    for _ in range(sh - 1):
        p = [_dot3(pi, pi) for pi in p]
        x = [xi + _dot3(xi, pi) for xi, pi in zip(x, p)]
    while blk < c:
        sh += 1
        mask = ((r >> sh) == (q >> sh)) & ((r >> (sh - 1)) != (q >> (sh - 1)))
        lx = [_dot3(jnp.where(mask, ai, 0.0), xi) for ai, xi in zip(a, x)]
        x = [xi - _dot3(xi, li) for xi, li in zip(x, lx)]
        blk *= 2
    return x


def _dn_kernel(q_ref, k_ref, v_ref, z_ref, ba_ref, bq_ref, bk_ref, bv_ref, s0_ref,
               cwq_ref, cwk_ref, cwv_ref, avec_ref, dtb_ref, og_ref,
               o_ref, sfin_ref, xq, xk, xv, state, *, tt, c, t_valid, nt):
    h = pl.program_id(1)
    t = pl.program_id(2)
    lead = SUBLANES

    @pl.when(t == 0)
    def _():
        state[...] = s0_ref[0, 0]
        xq[0:lead] = bq_ref[0]
        xk[0:lead] = bk_ref[0]
        xv[0:lead] = bv_ref[0]

    def conv_silu(xb, x_ref, cw_ref):
        xb[lead:lead + tt] = x_ref[0]
        y = xb[lead - 3:lead - 3 + tt] * cw_ref[0:1]
        for j in range(1, DN_CONV):
            y = y + xb[lead - 3 + j:lead - 3 + j + tt] * cw_ref[j:j + 1]
        xb[0:lead] = xb[tt:tt + lead]
        return _silu(y)

    q = conv_silu(xq, q_ref, cwq_ref)
    k = conv_silu(xk, k_ref, cwk_ref)
    v = conv_silu(xv, v_ref, cwv_ref)
    q = q * lax.rsqrt(jnp.sum(q * q, axis=-1, keepdims=True) + EPS) * (DN_DK ** -0.5)
    k = k * lax.rsqrt(jnp.sum(k * k, axis=-1, keepdims=True) + EPS)

    ba = ba_ref[0]
    lane = lax.broadcasted_iota(jnp.int32, (tt, LANES), 1)
    beta_all = jax.nn.sigmoid(ba)
    g_all = -avec_ref[...] * jax.nn.softplus(ba + dtb_ref[...])
    if t_valid < nt * tt:
        row = t * tt + lax.broadcasted_iota(jnp.int32, (tt, LANES), 0)
        beta_all = jnp.where(row < t_valid, beta_all, 0.0)
        g_all = jnp.where(row < t_valid, g_all, 0.0)
    beta = jnp.sum(jnp.where(lane == h, beta_all, 0.0), axis=1, keepdims=True)
    g_sel = jnp.where(lane == h + DN_HEADS, g_all, 0.0)

    ri = lax.broadcasted_iota(jnp.int32, (c, c), 0)
    ci = lax.broadcasted_iota(jnp.int32, (c, c), 1)
    causal = ri >= ci
    strict = ri > ci
    ltri = causal.astype(BF16)
    ones8 = jnp.ones((SUBLANES, LANES), BF16)
    og = og_ref[...]

    sls = [slice(ch * c, (ch + 1) * c) for ch in range(tt // c)]
    gcs = [_dot_sel(ltri, g_sel[sl]) for sl in sls]
    gcols = [jnp.sum(gc, axis=1, keepdims=True) for gc in gcs]
    grows = [_dot_sel(ones8, gc, nt=True)[0:1] for gc in gcs]
    egs = [jnp.exp(gcol) for gcol in gcols]
    kbs = [k[sl] * beta[sl] for sl in sls]
    kqs = [_dot_nt(jnp.concatenate([kb, q[sl]], axis=0).astype(BF16), k[sl].astype(BF16))
           for kb, sl in zip(kbs, sls)]
    gams = [jnp.exp(jnp.where(causal, gcol - grow, NEG)) for gcol, grow in zip(gcols, grows)]
    t_invs = _tri_inv([jnp.where(strict, kq[:c] * gam, 0.0) for kq, gam in zip(kqs, gams)], c)
    uws = [_dot(ti.astype(BF16), jnp.concatenate([v[sl] * beta[sl], kb * eg], axis=1).astype(BF16)).astype(BF16)
           for ti, sl, kb, eg in zip(t_invs, sls, kbs, egs)]
    qkuws = [_dot((kq[c:] * gam).astype(BF16), uw) for kq, gam, uw in zip(kqs, gams, uws)]
    glasts = [gcol[c - 1:c] for gcol in gcols]
    kduws = [lax.dot_general((k[sl] * jnp.exp(glast - gcol)).astype(BF16), uw, (((0,), (0,)), ((), ())),
                             preferred_element_type=F32)
             for sl, glast, gcol, uw in zip(sls, glasts, gcols, uws)]
    lhss = [jnp.concatenate([q[sl] * eg - qkuw[:, DN_DV:], kduw[:, DN_DV:]], axis=0).astype(BF16)
            for sl, eg, qkuw, kduw in zip(sls, egs, qkuws, kduws)]

    s = state[...]
    for sl, lhs, qkuw, kduw, glast in zip(sls, lhss, qkuws, kduws, glasts):
        r = _dot(lhs, s.astype(BF16))
        o = r[:c] + qkuw[:, :DN_DV]
        s = s * jnp.exp(glast) - r[c:] + kduw[:, :DN_DV]
        o = o * lax.rsqrt(jnp.mean(o * o, axis=-1, keepdims=True) + EPS) * og
        o_ref[0, sl, :] = o * _silu(z_ref[0, sl, :])
    state[...] = s

    @pl.when(t == nt - 1)
    def _():
        sfin_ref[0, 0] = state[...]


def deltanet_core(qkv, z, ba, buf, s0, conv_w, a_log, dt_bias, o_gain, *, tt, c, t_valid):
    b, tp, _ = qkv.shape
    hh = DN_HEADS
    nt = tp // tt
    assert tp % tt == 0 and tt % c == 0
    zpad = jnp.zeros((LANES - 2 * hh,), F32)
    avec = jnp.concatenate([jnp.zeros((hh,), F32), jnp.exp(a_log), zpad]).reshape(1, LANES)
    dtb = jnp.concatenate([jnp.zeros((hh,), F32), dt_bias, zpad]).reshape(1, LANES)
    col = lambda off: pl.BlockSpec((1, tt, DN_DK), lambda i, h, t: (i, t, off + h))
    bufc = lambda off: pl.BlockSpec((1, SUBLANES, DN_DK), lambda i, h, t: (i, 0, off + h))
    cwc = lambda off: pl.BlockSpec((DN_CONV, DN_DK), lambda i, h, t: (0, off + h))
    kern = functools.partial(_dn_kernel, tt=tt, c=c, t_valid=t_valid, nt=nt)
    return pl.pallas_call(
        kern,
        grid=(b, hh, nt),
        in_specs=[col(0), col(hh), col(2 * hh),
                  pl.BlockSpec((1, tt, DN_DV), lambda i, h, t: (i, t, h)),
                  pl.BlockSpec((1, tt, LANES), lambda i, h, t: (i, t, 0)),
                  bufc(0), bufc(hh), bufc(2 * hh),
                  pl.BlockSpec((1, 1, DN_DK, DN_DV), lambda i, h, t: (i, h, 0, 0)),
                  cwc(0), cwc(hh), cwc(2 * hh),
                  pl.BlockSpec((1, LANES), lambda i, h, t: (0, 0)),
                  pl.BlockSpec((1, LANES), lambda i, h, t: (0, 0)),
                  pl.BlockSpec((1, DN_DV), lambda i, h, t: (0, 0))],
        out_specs=[pl.BlockSpec((1, tt, DN_DV), lambda i, h, t: (i, t, h)),
                   pl.BlockSpec((1, 1, DN_DK, DN_DV), lambda i, h, t: (i, h, 0, 0))],
        out_shape=[jax.ShapeDtypeStruct((b, tp, hh * DN_DV), F32),
                   jax.ShapeDtypeStruct((b, hh, DN_DK, DN_DV), F32)],
        scratch_shapes=[pltpu.VMEM((SUBLANES + tt, DN_DK), F32), pltpu.VMEM((SUBLANES + tt, DN_DK), F32),
                        pltpu.VMEM((SUBLANES + tt, DN_DV), F32), pltpu.VMEM((DN_DK, DN_DV), F32)],
        compiler_params=_cparams(("parallel", "parallel", "arbitrary")),
        name="deltanet_core",
    )(qkv, qkv, qkv, z, ba, buf, buf, buf, s0, conv_w, conv_w, conv_w, avec, dtb, o_gain.reshape(1, DN_DV))


def _topk_mask_rows(g, n_valid, n_rows, k_top):
    jrow = lax.broadcasted_iota(jnp.int32, g.shape, 0)
    cnt = jnp.zeros(g.shape, F32)
    for jp in range(n_rows):
        gb = g[jp:jp + 1, :]
        beats = jnp.where(gb > g, 1.0, jnp.where(gb == g, jnp.where(jrow > jp, 1.0, 0.0), 0.0))
        cnt = cnt + jnp.where(jp < n_valid, beats, 0.0)
    return jnp.where(jrow < n_valid, cnt, float(k_top)) < float(k_top), cnt


LOG2E = 1.4426950408889634
N_EXT = 16


def _split3_f32(a):
    p1 = a.astype(BF16).astype(F32)
    r1 = a - p1
    p2 = r1.astype(BF16).astype(F32)
    return p1, p2, (r1 - p2).astype(BF16).astype(F32)


def _moba_kernel(q_ref, k_ref, v_ref, sl_ref, o_ref, kmean, kaug, vt, pen, sbuf, *, nb, nbp, grp):
    i = pl.program_id(2)
    blk, dh = MOBA_BLOCK, MOBA_DH

    @pl.when(i == 0)
    def _():
        kmean[...] = jnp.zeros_like(kmean)
        lane = lax.broadcasted_iota(jnp.int32, (blk, LANES), 1)
        crow = lax.broadcasted_iota(jnp.int32, (blk, LANES), 0).astype(F32)
        kext = jnp.where(lane < 3, crow, jnp.where(lane < 6, 1.0, 0.0)).astype(BF16)
        for j in range(nb):
            rows = slice(j * blk, (j + 1) * blk)
            kj = k_ref[0, rows, :]
            kmean[j:j + 1, :] = jnp.sum(kj, axis=0, keepdims=True) * (1.0 / blk)
            kaug[j] = jnp.concatenate([kj.astype(BF16), kext], axis=1)
            vt[j] = v_ref[0, rows, :].T.astype(BF16)

    slope = sl_ref[0]
    qt = (q_ref[0] * (dh ** -0.5)).T
    qtb = (qt * LOG2E).astype(BF16)
    gate = _dot3(kmean[...], qt)
    sel, _ = _topk_mask_rows(gate, i, nb, MOBA_TOPK)
    jrow = lax.broadcasted_iota(jnp.int32, (nbp, blk), 0)
    ridx = lax.broadcasted_iota(jnp.int32, (nbp, blk), 1)
    row_j = LOG2E * (jnp.where(sel, 0.0, NEG) - slope * ((i - jrow) * blk + ridx).astype(F32))
    for n, piece in enumerate(_split3_f32(row_j)):
        pen[n] = piece

    slope_pieces = _split3_f32(slope * LOG2E)
    row16 = lax.broadcasted_iota(jnp.int32, (N_EXT, blk), 0)
    zpad = jnp.zeros((LANES - N_EXT, blk), BF16)

    def q_aug(pieces):
        ext = jnp.zeros((N_EXT, blk), F32)
        for n, piece in enumerate(tuple(slope_pieces) + tuple(pieces)):
            ext = jnp.where(row16 == n, piece, ext)
        return jnp.concatenate([qtb, ext.astype(BF16), zpad], axis=0)

    key = lax.broadcasted_iota(jnp.int32, (blk, blk), 0)
    qry = lax.broadcasted_iota(jnp.int32, (blk, blk), 1)
    own_row = -LOG2E * slope * lax.broadcasted_iota(jnp.int32, (1, blk), 1).astype(F32)
    def fold8(x, op):
        return op(x.reshape(blk // SUBLANES, SUBLANES, blk), axis=0)

    s_own = jnp.where(key <= qry, _dot(kaug[i], q_aug(_split3_f32(own_row))), NEG)

    n_trips = (i + grp - 1) // grp

    def pass_scores(g, m8):
        for u in range(grp):
            j = g * grp + u
            s = _dot(kaug[j], q_aug([pen[n, pl.ds(j, 1), :] for n in range(3)]))
            sbuf[j] = s
            m8 = jnp.maximum(m8, fold8(s, jnp.max))
        return m8

    m8 = lax.fori_loop(0, n_trips, pass_scores, fold8(s_own, jnp.max))
    m = jnp.max(m8, axis=0, keepdims=True)

    p = jnp.exp2(s_own - m)
    l8 = fold8(p, jnp.sum)
    acc = _dot(vt[i], p.astype(BF16))

    def pass_values(g, carry):
        l8, acc = carry
        for u in range(grp):
            j = g * grp + u
            p = jnp.exp2(sbuf[j] - m)
            l8 = l8 + fold8(p, jnp.sum)
            acc = acc + _dot(vt[j], p.astype(BF16))
        return l8, acc

    l8, acc = lax.fori_loop(0, n_trips, pass_values, (l8, acc))
    o_ref[0] = (acc / jnp.sum(l8, axis=0, keepdims=True)).T


def _alibi_rows(n_heads, width):
    slopes = jnp.exp2(-8.0 * jnp.arange(1, n_heads + 1, dtype=F32) / n_heads)
    return jnp.broadcast_to(slopes[:, None, None], (n_heads, 1, width))


def moba_prompt_attn(q, k, v):
    b, t, w = q.shape
    hh, dh, blk = MOBA_HEADS, MOBA_DH, MOBA_BLOCK
    assert t % blk == 0
    nb = t // blk
    nbp = -(-nb // SUBLANES) * SUBLANES
    grp = next(g for g in (4, 2, 1) if nb % g == 0)
    kern = functools.partial(_moba_kernel, nb=nb, nbp=nbp, grp=grp)
    return pl.pallas_call(
        kern,
        grid=(b, hh, nb),
        in_specs=[pl.BlockSpec((1, blk, dh), lambda bi, h, i: (bi, i, h)),
                  pl.BlockSpec((1, t, dh), lambda bi, h, i: (bi, 0, h)),
                  pl.BlockSpec((1, t, dh), lambda bi, h, i: (bi, 0, h)),
                  pl.BlockSpec((1, 1, blk), lambda bi, h, i: (h, 0, 0))],
        out_specs=pl.BlockSpec((1, blk, dh), lambda bi, h, i: (bi, i, h)),
        out_shape=jax.ShapeDtypeStruct((b, t, w), F32),
        scratch_shapes=[pltpu.VMEM((nbp, dh), F32), pltpu.VMEM((nb, blk, dh + LANES), BF16),
                        pltpu.VMEM((nb, dh, blk), BF16),
                        pltpu.VMEM((3, nbp, blk), F32), pltpu.VMEM((nb, blk, blk), F32)],
        compiler_params=_cparams(("parallel", "parallel", "arbitrary")),
        name="moba_prompt_attn",
    )(q, k, v, _alibi_rows(hh, blk))


PAGES_PER_STEP = 8
PPB = MOBA_BLOCK // PAGE_SIZE


def _kmean_kernel(pt_ref, *refs):
    page_refs, o_ref = refs[:PAGES_PER_STEP], refs[PAGES_PER_STEP]
    for u in range(0, PAGES_PER_STEP, PPB):
        tot = jnp.sum(page_refs[u][...], axis=0)
        for w in range(1, PPB):
            tot = tot + jnp.sum(page_refs[u + w][...], axis=0)
        o_ref[0, u // PPB] = tot * (1.0 / MOBA_BLOCK)


def moba_block_means(pool_k, layer, page_table):
    n_seq, n_pages = page_table.shape
    _, _, page, hh, dh = pool_k.shape
    assert n_pages % PAGES_PER_STEP == 0 and PAGES_PER_STEP % PPB == 0
    steps = n_pages // PAGES_PER_STEP

    def page_spec(u):
        return pl.BlockSpec((None, None, page, hh, dh),
                            lambda b, g, pt: (layer, pt[b * n_pages + g * PAGES_PER_STEP + u], 0, 0, 0))

    bps = PAGES_PER_STEP // PPB
    return pl.pallas_call(
        _kmean_kernel,
        grid_spec=pltpu.PrefetchScalarGridSpec(
            num_scalar_prefetch=1, grid=(n_seq, steps),
            in_specs=[page_spec(u) for u in range(PAGES_PER_STEP)],
            out_specs=pl.BlockSpec((1, bps, hh, dh), lambda b, g, pt: (b, g, 0, 0))),
        out_shape=jax.ShapeDtypeStruct((n_seq, n_pages // PPB, hh, dh), F32),
        compiler_params=_cparams(("parallel", "arbitrary")),
        name="moba_block_means",
    )(page_table.reshape(-1), *([pool_k] * PAGES_PER_STEP))


def _moba_pick_kernel(q_ref, km_ref, o_ref, *, n_blk, t):
    hh, dh = MOBA_HEADS, MOBA_DH
    q = q_ref[0] * (dh ** -0.5)
    cols = []
    for h in range(hh):
        sl = slice(h * dh, (h + 1) * dh)
        cols.append(_dot3(km_ref[0, :, sl], q[:, sl], nt=True))
    gate = jnp.concatenate(cols, axis=1)
    _, cnt = _topk_mask_rows(gate, n_blk, n_blk, MOBA_TOPK)
    jrow = lax.broadcasted_iota(jnp.int32, gate.shape, 0).astype(F32)
    rows = [jnp.sum(jnp.where(cnt == float(r), jrow, 0.0), axis=0, keepdims=True)
            for r in range(MOBA_TOPK)]
    rows.append(jnp.zeros((SUBLANES - MOBA_TOPK, hh * t), F32))
    o_ref[0] = jnp.concatenate(rows, axis=0).astype(jnp.int32)


def moba_pick_blocks(q, kmean):
    n_seq, t, w = q.shape
    n_blk = kmean.shape[1]
    kern = functools.partial(_moba_pick_kernel, n_blk=n_blk, t=t)
    return pl.pallas_call(
        kern,
        grid=(n_seq,),
        in_specs=[pl.BlockSpec((1, t, w), lambda b: (b, 0, 0)),
                  pl.BlockSpec((1, n_blk, w), lambda b: (b, 0, 0))],
        out_specs=pl.BlockSpec((1, SUBLANES, MOBA_HEADS * t), lambda b: (b, 0, 0)),
        out_shape=jax.ShapeDtypeStruct((n_seq, SUBLANES, MOBA_HEADS * t), jnp.int32),
        compiler_params=_cparams(("parallel",)),
        name="moba_pick_blocks",
    )(q, kmean)


def _moba_dec_kernel(pt_ref, sel_ref, q_ref, kn_ref, vn_ref, sl_ref, pk_hbm, pv_hbm, o_ref,
                     kbuf, vbuf, sem, *, layer, t_len, past, n_pages, n_heads, n_steps):
    n_pg = MOBA_TOPK * PPB
    b, h = pl.program_id(0), pl.program_id(1)
    step = b * n_heads + h
    slot = step % 2
    page = PAGE_SIZE

    def blk_of(bb, hh, t, r):
        return sel_ref[((bb * MOBA_TOPK + r) * n_heads + hh) * t_len + t]

    def copies(bb, hh, sl):
        out = []
        for t in range(t_len):
            for r in range(MOBA_TOPK):
                for u in range(PPB):
                    phys = pt_ref[bb * n_pages + blk_of(bb, hh, t, r) * PPB + u]
                    i = (t * MOBA_TOPK + r) * PPB + u
                    out.append(pltpu.make_async_copy(pk_hbm.at[layer, phys, :, hh, :], kbuf.at[sl, i], sem.at[0, sl]))
                    out.append(pltpu.make_async_copy(pv_hbm.at[layer, phys, :, hh, :], vbuf.at[sl, i], sem.at[1, sl]))
        return out

    @pl.when(step == 0)
    def _():
        for cp in copies(b, h, slot):
            cp.start()

    @pl.when(step + 1 < n_steps)
    def _():
        nxt = step + 1
        for cp in copies(nxt // n_heads, nxt % n_heads, 1 - slot):
            cp.start()

    slope = sl_ref[0][:, :page]
    q = (q_ref[0] * (MOBA_DH ** -0.5)).astype(BF16)
    row = lax.broadcasted_iota(jnp.int32, (t_len, page), 0)
    lane = lax.broadcasted_iota(jnp.int32, (t_len, page), 1)
    s_own = _dot_nt(q, kn_ref[0].astype(BF16))
    qi = lax.broadcasted_iota(jnp.int32, (t_len, t_len), 0)
    ki = lax.broadcasted_iota(jnp.int32, (t_len, t_len), 1)
    s_own = jnp.where(ki <= qi, s_own - slope[:, :t_len] * (qi - ki).astype(F32), NEG)

    for cp in copies(b, h, slot):
        cp.wait()

    scores = [s_own]
    for t in range(t_len):
        for r in range(MOBA_TOPK):
            for u in range(PPB):
                dist = (past + row) - (blk_of(b, h, t, r) * MOBA_BLOCK + u * page + lane)
                s = _dot_nt(q, kbuf[slot, (t * MOBA_TOPK + r) * PPB + u].astype(BF16))
                scores.append(jnp.where(row == t, s - slope * dist.astype(F32), NEG))
    m = scores[0].max(axis=-1, keepdims=True)
    for s in scores[1:]:
        m = jnp.maximum(m, s.max(axis=-1, keepdims=True))
    ps = [jnp.exp(s - m) for s in scores]
    l = ps[0].sum(axis=-1, keepdims=True)
    for p in ps[1:]:
        l = l + p.sum(axis=-1, keepdims=True)
    inv_l = 1.0 / l
    acc = _dot((ps[0] * inv_l).astype(BF16), vn_ref[0].astype(BF16))
    for i in range(t_len * n_pg):
        acc = acc + _dot((ps[1 + i] * inv_l).astype(BF16), vbuf[slot, i].astype(BF16))
    o_ref[0, 0] = acc


def moba_decode_attn(q, k_new, v_new, pool_k, pool_v, layer, page_table, sel):
    n_seq, t_len, w = q.shape
    n_pages = page_table.shape[1]
    hh, dh = MOBA_HEADS, MOBA_DH
    _, _, page, _, _ = pool_k.shape
    past = n_pages * page
    assert past % MOBA_BLOCK == 0
    n_buf = t_len * MOBA_TOPK * PPB
    row = lambda: pl.BlockSpec((1, t_len, dh), lambda b, h, pt, sl: (b, 0, h))
    kern = functools.partial(_moba_dec_kernel, layer=layer, t_len=t_len, past=past, n_pages=n_pages,
                             n_heads=hh, n_steps=n_seq * hh)
    return pl.pallas_call(
        kern,
        grid_spec=pltpu.PrefetchScalarGridSpec(
            num_scalar_prefetch=2, grid=(n_seq, hh),
            in_specs=[row(), row(), row(),
                      pl.BlockSpec((1, 1, MOBA_BLOCK), lambda b, h, pt, sl: (h, 0, 0)),
                      pl.BlockSpec(memory_space=pl.ANY), pl.BlockSpec(memory_space=pl.ANY)],
            out_specs=pl.BlockSpec((1, 1, t_len, dh), lambda b, h, pt, sl: (b, h, 0, 0)),
            scratch_shapes=[pltpu.VMEM((2, n_buf, page, dh), F32), pltpu.VMEM((2, n_buf, page, dh), F32),
                            pltpu.SemaphoreType.DMA((2, 2))]),
        out_shape=jax.ShapeDtypeStruct((n_seq, hh, t_len, dh), F32),
        compiler_params=_cparams(("arbitrary", "arbitrary")),
        name="moba_decode_attn",
    )(page_table.reshape(-1), sel.reshape(-1), q, k_new, v_new, _alibi_rows(hh, MOBA_BLOCK), pool_k, pool_v)


def _last_rows(buf, x, n):
    t = x.shape[1]
    if t >= n:
        return x[:, t - n:]
    return jnp.concatenate([buf[:, buf.shape[1] - (n - t):], x], axis=1)


def _pad_front(buf, rows):
    return jnp.pad(buf, ((0, 0), (rows - buf.shape[1], 0), (0, 0)))


def kernel(x_prompt, x_sample, state_dn_s, state_dn_conv, cache_moba_k, cache_moba_v, page_table, cache_mem_k, cache_mem_v, state_ffn_conv, mem_prompt, norm_mix, norm_xattn, norm_mem, norm_ffn, norm_final, dn_w_in, dn_conv_w, dn_a_log, dn_dt_bias, dn_o_gain, dn_w_out, moba_w_qkv, moba_w_out, xa_w_q, xa_w_kv, xa_w_out, ffn_w_up, ffn_conv_w, ffn_w_down):
    bp, seq, d = x_prompt.shape
    bs, dseq, _ = x_sample.shape
    depth = norm_mix.shape[0]
    mem_len = mem_prompt.shape[1]
    d_ff = ffn_w_down.shape[1]
    qk_w = DN_HEADS * DN_DK
    v_w = DN_HEADS * DN_DV
    conv_ch = 2 * qk_w + v_w
    moba_w = MOBA_HEADS * MOBA_DH
    xa_w = XA_HEADS * XA_DH
    bf = lambda a: a.astype(BF16)

    hp = x_prompt.reshape(bp * seq, d)
    hs = x_sample.reshape(bs * dseq, d)
    outs = {k: [] for k in ("p_dn_s", "p_dn_c", "p_mk", "p_mv", "p_memk", "p_memv", "p_ffc",
                            "s_dn_s", "s_dn_c", "s_mk", "s_mv", "s_ffc")}
    for layer in range(depth):
        if layer % 2 == 0:
            ia = layer // 2
            w_in = dn_w_in[ia]
            ws = [bf(w_in[:, :conv_ch]), bf(w_in[:, conv_ch:conv_ch + v_w]),
                  bf(jnp.pad(w_in[:, conv_ch + v_w:], ((0, 0), (0, LANES - 2 * DN_HEADS))))]
            w_out = bf(dn_w_out[ia])
            args = (dn_conv_w[ia], dn_a_log[ia], dn_dt_bias[ia], dn_o_gain[ia])
            qkv, z, ba = norm_proj(hp, norm_mix[layer], ws, tm=512)
            qkv3 = qkv.reshape(bp, seq, conv_ch)
            o, s_fin = deltanet_core(qkv3, z.reshape(bp, seq, v_w), ba.reshape(bp, seq, LANES),
                                     jnp.zeros((bp, SUBLANES, conv_ch), F32),
                                     jnp.zeros((bp, DN_HEADS, DN_DK, DN_DV), F32), *args,
                                     tt=512, c=DN_CHUNK, t_valid=seq)
            hp = out_proj_res(o.reshape(bp * seq, v_w), w_out, hp, tm=512)
            outs["p_dn_s"].append(s_fin)
            outs["p_dn_c"].append(_last_rows(jnp.zeros((bp, DN_CONV - 1, conv_ch), F32), qkv3, DN_CONV - 1))
            qkv, z, ba = norm_proj(hs, norm_mix[layer], ws, tm=256)
            qkv3 = qkv.reshape(bs, dseq, conv_ch)
            tpad = 16
            padt = lambda a: jnp.pad(a.reshape(bs, dseq, -1), ((0, 0), (0, tpad - dseq), (0, 0)))
            o, s_fin = deltanet_core(padt(qkv), padt(z), padt(ba),
                                     _pad_front(state_dn_conv[ia], SUBLANES), state_dn_s[ia], *args,
                                     tt=tpad, c=tpad, t_valid=dseq)
            hs = out_proj_res(o[:, :dseq].reshape(bs * dseq, v_w), w_out, hs, tm=256)
            outs["s_dn_s"].append(s_fin)
            outs["s_dn_c"].append(_last_rows(state_dn_conv[ia], qkv3, DN_CONV - 1))
        else:
            ib = layer // 2
            w_qkv = moba_w_qkv[ib]
            ws = [bf(w_qkv[:, j * moba_w:(j + 1) * moba_w]) for j in range(3)]
            w_out = bf(moba_w_out[ib])
            q, k, v = (a.reshape(bp, seq, moba_w) for a in norm_proj(hp, norm_mix[layer], ws, tm=512))
            o = moba_prompt_attn(q, k, v)
            hp = out_proj_res(o.reshape(bp * seq, moba_w), w_out, hp, tm=512)
            outs["p_mk"].append(k.reshape(bp, seq, MOBA_HEADS, MOBA_DH))
            outs["p_mv"].append(v.reshape(bp, seq, MOBA_HEADS, MOBA_DH))
            q, k, v = (a.reshape(bs, dseq, moba_w) for a in norm_proj(hs, norm_mix[layer], ws, tm=256))
            kmean = moba_block_means(cache_moba_k, ib, page_table)
            pick = moba_pick_blocks(q, kmean.reshape(bs, -1, moba_w))
            sel = pick[:, :MOBA_TOPK].reshape(bs, MOBA_TOPK, MOBA_HEADS, dseq)
            o = moba_decode_attn(q, k, v, cache_moba_k, cache_moba_v, ib, page_table, sel)
            o = o.transpose(0, 2, 1, 3).reshape(bs * dseq, moba_w)
            hs = out_proj_res(o, w_out, hs, tm=256)
            outs["s_mk"].append(k.reshape(bs, dseq, MOBA_HEADS, MOBA_DH))
            outs["s_mv"].append(v.reshape(bs, dseq, MOBA_HEADS, MOBA_DH))

        w_kv = xa_w_kv[layer]
        mk, mv = norm_proj(mem_prompt.reshape(bp * mem_len, d), norm_mem[layer],
                           [bf(w_kv[:, :xa_w]), bf(w_kv[:, xa_w:])], tm=256)
        mk = mk.reshape(bp, mem_len, xa_w)
        mv = mv.reshape(bp, mem_len, xa_w)
        outs["p_memk"].append(mk.reshape(bp, mem_len, XA_HEADS, XA_DH))
        outs["p_memv"].append(mv.reshape(bp, mem_len, XA_HEADS, XA_DH))
        w_q, w_o = bf(xa_w_q[layer]), bf(xa_w_out[layer])
        hp = mem_xattn(hp, mk, mv, norm_xattn[layer], w_q, w_o, nb=1, tm=512)
        hs = mem_xattn(hs, cache_mem_k[layer].reshape(bs, mem_len, xa_w),
                       cache_mem_v[layer].reshape(bs, mem_len, xa_w),
                       norm_xattn[layer], w_q, w_o, nb=8, tm=dseq)

        last = layer == depth - 1
        w_up, w_dn = bf(ffn_w_up[layer]), bf(ffn_w_down[layer])
        hp3, tail = conv_ffn(hp.reshape(bp, seq, d), jnp.zeros((bp, SUBLANES, 2 * d_ff), F32),
                             norm_ffn[layer], w_up, ffn_conv_w[layer], w_dn, norm_final,
                             tm=512, stride=1, final_norm=last)
        hp = hp3.reshape(bp * seq, d)
        outs["p_ffc"].append(tail[:, SUBLANES - (FFN_CONV - 1):])
        hs_tm = hs.reshape(bs, dseq, d).transpose(1, 0, 2).reshape(1, dseq * bs, d)
        buf_tm = state_ffn_conv[layer].transpose(1, 0, 2).reshape(1, (FFN_CONV - 1) * bs, 2 * d_ff)
        hs3, tail = conv_ffn(hs_tm, buf_tm, norm_ffn[layer], w_up, ffn_conv_w[layer], w_dn, norm_final,
                             tm=dseq * bs, stride=bs, final_norm=last)
        hs = hs3.reshape(dseq, bs, d).transpose(1, 0, 2).reshape(bs * dseq, d)
        outs["s_ffc"].append(tail.reshape(FFN_CONV - 1, bs, 2 * d_ff).transpose(1, 0, 2))

    y_prompt = hp.reshape(bp, seq, d)
    y_sample = hs.reshape(bs, dseq, d)
    st = lambda k: outs[k][0][None] if len(outs[k]) == 1 else jnp.stack(outs[k])
    return (y_prompt, y_sample, st("p_dn_s"), st("p_dn_c"), st("p_mk"), st("p_mv"),
            st("p_memk"), st("p_memv"), st("p_ffc"),
            st("s_dn_s"), st("s_dn_c"), st("s_mk"), st("s_mv"), st("s_ffc"))
```

```python
import functools

import jax
import jax.numpy as jnp
from jax import lax
from jax.experimental import pallas as pl
from jax.experimental.pallas import tpu as pltpu

F32 = jnp.float32
BF16 = jnp.bfloat16
EPS = 1e-6
NEG = -1e30

DN_HEADS = 8
DN_DK = 128
DN_DV = 128
DN_CONV = 4
DN_CHUNK = 64
MOBA_HEADS = 8
MOBA_DH = 128
MOBA_BLOCK = 256
MOBA_TOPK = 3
PAGE_SIZE = 128
XA_HEADS = 4
XA_DH = 128
FFN_CONV = 3

SUBLANES = 8
LANES = 128
VMEM_LIMIT = 56 * 1024 * 1024


def _cparams(sem):
    return pltpu.CompilerParams(dimension_semantics=sem, vmem_limit_bytes=VMEM_LIMIT)


def _resident(shape):
    nd = len(shape)
    return pl.BlockSpec(shape, lambda *_: (0,) * nd, pipeline_mode=pl.Buffered(1))


def _rms(x, gain):
    return x * lax.rsqrt(jnp.mean(x * x, axis=-1, keepdims=True) + EPS) * gain


def _silu(x):
    return x * jax.nn.sigmoid(x)


def _dot(a, b):
    return jnp.dot(a, b, preferred_element_type=F32)


def _dot_nt(a, b):
    return lax.dot_general(a, b, (((1,), (1,)), ((), ())), preferred_element_type=F32)


def _split2(a):
    hi = a.astype(BF16)
    lo = (a - hi.astype(F32)).astype(BF16)
    return hi, lo


def _dot3(a, b, nt=False):
    d = _dot_nt if nt else _dot
    ah, al = _split2(a)
    bh, bl = _split2(b)
    return d(ah, bh) + (d(al, bh) + d(ah, bl))


def _split3(a):
    p1 = a.astype(BF16)
    r1 = a - p1.astype(F32)
    p2 = r1.astype(BF16)
    r2 = r1 - p2.astype(F32)
    return p1, p2, r2.astype(BF16)


def _dot_sel(sel, x, nt=False):
    d = _dot_nt if nt else _dot
    p1, p2, p3 = _split3(x)
    return d(sel, p1) + (d(sel, p2) + d(sel, p3))


def _norm_proj_kernel(x_ref, g_ref, *refs, n_w):
    w_refs, o_refs = refs[:n_w], refs[n_w:]
    xn = _rms(x_ref[...], g_ref[...]).astype(BF16)
    for w_ref, o_ref in zip(w_refs, o_refs):
        n = w_ref.shape[1]
        for c in range(0, n, 512):
            cw = min(512, n - c)
            o_ref[:, c:c + cw] = _dot(xn, w_ref[:, c:c + cw])


def norm_proj(x, gain, ws, tm):
    r, d = x.shape
    tm = min(tm, r)
    assert r % tm == 0
    n_w = len(ws)
    return pl.pallas_call(
        functools.partial(_norm_proj_kernel, n_w=n_w),
        grid=(r // tm,),
        in_specs=[pl.BlockSpec((tm, d), lambda i: (i, 0)), _resident((1, d))]
        + [_resident(w.shape) for w in ws],
        out_specs=[pl.BlockSpec((tm, w.shape[1]), lambda i: (i, 0)) for w in ws],
        out_shape=[jax.ShapeDtypeStruct((r, w.shape[1]), F32) for w in ws],
        compiler_params=_cparams(("parallel",)),
        name="norm_proj",
    )(x, gain.reshape(1, d), *ws)


def _out_proj_kernel(a_ref, w_ref, h_ref, o_ref):
    o_ref[...] = h_ref[...] + _dot(a_ref[...].astype(BF16), w_ref[...])


def out_proj_res(a, w, h, tm):
    r, k = a.shape
    d = w.shape[1]
    tm = min(tm, r)
    assert r % tm == 0
    return pl.pallas_call(
        _out_proj_kernel,
        grid=(r // tm,),
        in_specs=[pl.BlockSpec((tm, k), lambda i: (i, 0)), _resident(w.shape),
                  pl.BlockSpec((tm, d), lambda i: (i, 0))],
        out_specs=pl.BlockSpec((tm, d), lambda i: (i, 0)),
        out_shape=jax.ShapeDtypeStruct((r, d), F32),
        compiler_params=_cparams(("parallel",)),
        name="out_proj_res",
    )(a, w, h)


def _shift_rows(u, head, stride):
    rows = u.shape[0]
    if stride % SUBLANES == 0:
        return jnp.concatenate([head, u[:rows - stride]], axis=0)
    assert stride == 1
    rolled = pltpu.roll(u, 1, axis=0)
    first = jnp.where(lax.broadcasted_iota(jnp.int32, (SUBLANES, u.shape[1]), 0) == 0, head, rolled[:SUBLANES])
    return jnp.concatenate([first, rolled[SUBLANES:]], axis=0)


def _ffn_kernel(x_ref, buf_ref, g_ref, wup_ref, cw_ref, wdn_ref, fg_ref, o_ref, tail_ref, carry, hid,
                *, tm, stride, pad, d_ff, final_norm, chunk):
    t = pl.program_id(1)

    @pl.when(t == 0)
    def _():
        carry[...] = buf_ref[0]

    x = x_ref[0]
    xn = _rms(x, g_ref[...]).astype(BF16)
    for j0 in range(0, d_ff, chunk):
        cs = []
        for c0 in (j0, d_ff + j0):
            cols = slice(c0, c0 + chunk)
            u = _dot(xn, wup_ref[:, cols])
            prev = carry[:, cols]
            u1 = _shift_rows(u, prev[pad - stride:], stride)
            u2 = _shift_rows(u1, prev[pad - 2 * stride:pad - stride], stride)
            cs.append(u2 * cw_ref[0:1, cols] + u1 * cw_ref[1:2, cols] + u * cw_ref[2:3, cols])
            carry[:, cols] = u[tm - pad:]
        hid[:, j0:j0 + chunk] = (_silu(cs[0]) * cs[1]).astype(BF16)
    out = x + _dot(hid[...], wdn_ref[...])
    if final_norm:
        out = _rms(out, fg_ref[...])
    o_ref[0] = out
    tail_ref[0] = carry[...]


def conv_ffn(h, buf, gain, w_up, conv_w, w_down, final_gain, *, tm, stride, final_norm):
    nb, t, d = h.shape
    pad = buf.shape[1]
    d_ff = w_down.shape[0]
    tm = min(tm, t)
    assert t % tm == 0 and pad % SUBLANES == 0 and pad >= 2 * stride and tm >= pad
    kern = functools.partial(_ffn_kernel, tm=tm, stride=stride, pad=pad, d_ff=d_ff,
                             final_norm=final_norm, chunk=256)
    return pl.pallas_call(
        kern,
        grid=(nb, t // tm),
        in_specs=[pl.BlockSpec((1, tm, d), lambda b, i: (b, i, 0)),
                  pl.BlockSpec((1, pad, 2 * d_ff), lambda b, i: (b, 0, 0)),
                  _resident((1, d)), _resident(w_up.shape), _resident(conv_w.shape),
                  _resident(w_down.shape), _resident((1, d))],
        out_specs=[pl.BlockSpec((1, tm, d), lambda b, i: (b, i, 0)),
                   pl.BlockSpec((1, pad, 2 * d_ff), lambda b, i: (b, 0, 0))],
        out_shape=[jax.ShapeDtypeStruct((nb, t, d), F32),
                   jax.ShapeDtypeStruct((nb, pad, 2 * d_ff), F32)],
        scratch_shapes=[pltpu.VMEM((pad, 2 * d_ff), F32), pltpu.VMEM((tm, d_ff), BF16)],
        compiler_params=_cparams(("parallel", "arbitrary")),
        name="conv_ffn",
    )(h, buf, gain.reshape(1, d), w_up, conv_w, w_down, final_gain.reshape(1, d))


def _xattn_kernel(x_ref, mk_ref, mv_ref, g_ref, wq_ref, wo_ref, o_ref, att, *, nb, tm):
    x = x_ref[...]
    xn = _rms(x, g_ref[...]).astype(BF16)
    q = _dot(xn, wq_ref[...])
    scale = XA_DH ** -0.5
    for b in range(nb):
        rows = slice(b * tm, (b + 1) * tm)
        for hh in range(XA_HEADS):
            cols = slice(hh * XA_DH, (hh + 1) * XA_DH)
            s = _dot_nt(q[rows, cols].astype(BF16), mk_ref[b, :, cols].astype(BF16)) * scale
            p = jnp.exp(s - jnp.max(s, axis=-1, keepdims=True))
            p = p / jnp.sum(p, axis=-1, keepdims=True)
            att[rows, cols] = _dot(p.astype(BF16), mv_ref[b, :, cols].astype(BF16))
    o_ref[...] = x + _dot(att[...].astype(BF16), wo_ref[...])


def mem_xattn(h, mem_k, mem_v, gain, w_q, w_o, *, nb, tm):
    r, d = h.shape
    n_seq, m, w = mem_k.shape
    t = r // n_seq
    assert t % tm == 0 and (nb == 1 or tm == t) and n_seq % nb == 0
    tiles = t // tm
    kern = functools.partial(_xattn_kernel, nb=nb, tm=tm)
    return pl.pallas_call(
        kern,
        grid=(n_seq // nb, tiles),
        in_specs=[pl.BlockSpec((nb * tm, d), lambda b, i: (b * tiles + i, 0)),
                  pl.BlockSpec((nb, m, w), lambda b, i: (b, 0, 0)),
                  pl.BlockSpec((nb, m, w), lambda b, i: (b, 0, 0)),
                  _resident((1, d)), _resident(w_q.shape), _resident(w_o.shape)],
        out_specs=pl.BlockSpec((nb * tm, d), lambda b, i: (b * tiles + i, 0)),
        out_shape=jax.ShapeDtypeStruct((r, d), F32),
        scratch_shapes=[pltpu.VMEM((nb * tm, w), F32)],
        compiler_params=_cparams(("parallel", "arbitrary")),
        name="mem_xattn",
    )(h, mem_k, mem_v, gain.reshape(1, d), w_q, w_o)


def _tri_inv(a, c):
    r = lax.broadcasted_iota(jnp.int32, (c, c), 0)
    q = lax.broadcasted_iota(jnp.int32, (c, c), 1)
    eye = (r == q).astype(F32)
    x = [eye - jnp.where((r >> 1) == (q >> 1), ai, 0.0) for ai in a]
    sh = 1
    while (1 << sh) < c:
        sh += 1
        mask = ((r >> sh) == (q >> sh)) & ((r >> (sh - 1)) != (q >> (sh - 1)))
        xb = [xi.astype(BF16) for xi in x]
        lx = [_dot(jnp.where(mask, ai, 0.0).astype(BF16), xbi) for ai, xbi in zip(a, xb)]
        x = [xi - _dot(xbi, li.astype(BF16)) for xi, xbi, li in zip(x, xb, lx)]
    return x


def _dn_kernel(q_ref, k_ref, v_ref, z_ref, ba_ref, bq_ref, bk_ref, bv_ref, s0_ref,
               cwq_ref, cwk_ref, cwv_ref, avec_ref, dtb_ref, og_ref,
               o_ref, sfin_ref, xq, xk, xv, state, *, tt, c, t_valid, nt, hps):
    hg = pl.program_id(1)
    t = pl.program_id(2)
    lead = SUBLANES

    @pl.when(t == 0)
    def _():
        state[...] = s0_ref[0]
        xq[...] = bq_ref[0]
        xk[...] = bk_ref[0]
        xv[...] = bv_ref[0]

    def conv_silu(hist, x_ref, cw_ref):
        x = x_ref[0]
        taps = [x]
        for j in range(1, DN_CONV):
            taps.append(_shift_rows(taps[-1], hist[lead - j:lead - j + 1], 1))
        y = taps[DN_CONV - 1] * cw_ref[0:1]
        for j in range(1, DN_CONV):
            y = y + taps[DN_CONV - 1 - j] * cw_ref[j:j + 1]
        hist[...] = x[tt - lead:]
        return _silu(y)

    qa = conv_silu(xq, q_ref, cwq_ref)
    ka = conv_silu(xk, k_ref, cwk_ref)
    va = conv_silu(xv, v_ref, cwv_ref)

    ba = ba_ref[0]
    lane = lax.broadcasted_iota(jnp.int32, (tt, LANES), 1)
    beta_all = jax.nn.sigmoid(ba)
    g_all = -avec_ref[...] * jax.nn.softplus(ba + dtb_ref[...])
    if t_valid < nt * tt:
        row = t * tt + lax.broadcasted_iota(jnp.int32, (tt, LANES), 0)
        beta_all = jnp.where(row < t_valid, beta_all, 0.0)
        g_all = jnp.where(row < t_valid, g_all, 0.0)

    ri = lax.broadcasted_iota(jnp.int32, (c, c), 0)
    ci = lax.broadcasted_iota(jnp.int32, (c, c), 1)
    causal = ri >= ci
    strict = ri > ci
    ltri = causal.astype(BF16)
    ones8 = jnp.ones((SUBLANES, LANES), BF16)
    og = og_ref[...]

    q, k, v, beta, g_sel = [], [], [], [], []
    for hh in range(hps):
        cols = slice(hh * DN_DK, (hh + 1) * DN_DK)
        head = hg * hps + hh
        qh, kh = qa[:, cols], ka[:, cols]
        q.append(qh * lax.rsqrt(jnp.sum(qh * qh, axis=-1, keepdims=True) + EPS) * (DN_DK ** -0.5))
        k.append(kh * lax.rsqrt(jnp.sum(kh * kh, axis=-1, keepdims=True) + EPS))
        v.append(va[:, hh * DN_DV:(hh + 1) * DN_DV])
        beta.append(jnp.sum(jnp.where(lane == head, beta_all, 0.0), axis=1, keepdims=True))
        g_sel.append(jnp.where(lane == head + DN_HEADS, g_all, 0.0))

    n_ch = tt // c
    items = [(hh, slice(ch * c, (ch + 1) * c)) for ch in range(n_ch) for hh in range(hps)]
    gcs = [_dot_sel(ltri, g_sel[hh][sl]) for hh, sl in items]
    gcols = [jnp.sum(gc, axis=1, keepdims=True) for gc in gcs]
    grows = [_dot_sel(ones8, gc, nt=True)[0:1] for gc in gcs]
    egs = [jnp.exp(gcol) for gcol in gcols]
    kbs = [k[hh][sl] * beta[hh][sl] for hh, sl in items]
    kqs = [_dot_nt(jnp.concatenate([kb, q[hh][sl]], axis=0).astype(BF16), k[hh][sl].astype(BF16))
           for kb, (hh, sl) in zip(kbs, items)]
    gams = [jnp.exp(jnp.where(causal, gcol - grow, NEG)) for gcol, grow in zip(gcols, grows)]
    t_invs = _tri_inv([jnp.where(strict, kq[:c] * gam, 0.0) for kq, gam in zip(kqs, gams)], c)
    uws = [_dot(ti.astype(BF16),
                jnp.concatenate([v[hh][sl] * beta[hh][sl], kb * eg], axis=1).astype(BF16)).astype(BF16)
           for ti, (hh, sl), kb, eg in zip(t_invs, items, kbs, egs)]
    qkuws = [_dot((kq[c:] * gam).astype(BF16), uw) for kq, gam, uw in zip(kqs, gams, uws)]
    glasts = [gcol[c - 1:c] for gcol in gcols]
    kduws = [lax.dot_general((k[hh][sl] * jnp.exp(glast - gcol)).astype(BF16), uw, (((0,), (0,)), ((), ())),
                             preferred_element_type=F32)
             for (hh, sl), glast, gcol, uw in zip(items, glasts, gcols, uws)]
    lhss = [jnp.concatenate([q[hh][sl] * eg - qkuw[:, DN_DV:], kduw[:, DN_DV:]], axis=0).astype(BF16)
            for (hh, sl), eg, qkuw, kduw in zip(items, egs, qkuws, kduws)]

    s = [state[hh] for hh in range(hps)]
    for (hh, sl), lhs, qkuw, kduw, glast in zip(items, lhss, qkuws, kduws, glasts):
        r = _dot(lhs, s[hh].astype(BF16))
        o = r[:c] + qkuw[:, :DN_DV]
        s[hh] = s[hh] * jnp.exp(glast) - r[c:] + kduw[:, :DN_DV]
        o = o * lax.rsqrt(jnp.mean(o * o, axis=-1, keepdims=True) + EPS) * og
        cols = slice(hh * DN_DV, (hh + 1) * DN_DV)
        o_ref[0, sl, cols] = o * _silu(z_ref[0, sl, cols])
    for hh in range(hps):
        state[hh] = s[hh]

    @pl.when(t == nt - 1)
    def _():
        sfin_ref[0] = state[...]


def deltanet_core(qkv, z, ba, buf, s0, conv_w, a_log, dt_bias, o_gain, *, tt, c, t_valid, hps):
    b, tp, _ = qkv.shape
    hh = DN_HEADS
    nt = tp // tt
    assert tp % tt == 0 and tt % c == 0 and hh % hps == 0
    ng = hh // hps
    zpad = jnp.zeros((LANES - 2 * hh,), F32)
    avec = jnp.concatenate([jnp.zeros((hh,), F32), jnp.exp(a_log), zpad]).reshape(1, LANES)
    dtb = jnp.concatenate([jnp.zeros((hh,), F32), dt_bias, zpad]).reshape(1, LANES)
    wq, wv = hps * DN_DK, hps * DN_DV
    col = lambda part: pl.BlockSpec((1, tt, wq), lambda i, g, t: (i, t, part * ng + g))
    bufc = lambda part: pl.BlockSpec((1, SUBLANES, wq), lambda i, g, t: (i, 0, part * ng + g))
    cwc = lambda part: pl.BlockSpec((DN_CONV, wq), lambda i, g, t: (0, part * ng + g))
    kern = functools.partial(_dn_kernel, tt=tt, c=c, t_valid=t_valid, nt=nt, hps=hps)
    return pl.pallas_call(
        kern,
        grid=(b, ng, nt),
        in_specs=[col(0), col(1), col(2),
                  pl.BlockSpec((1, tt, wv), lambda i, g, t: (i, t, g)),
                  pl.BlockSpec((1, tt, LANES), lambda i, g, t: (i, t, 0)),
                  bufc(0), bufc(1), bufc(2),
                  pl.BlockSpec((1, hps, DN_DK, DN_DV), lambda i, g, t: (i, g, 0, 0)),
                  cwc(0), cwc(1), cwc(2),
                  pl.BlockSpec((1, LANES), lambda i, g, t: (0, 0)),
                  pl.BlockSpec((1, LANES), lambda i, g, t: (0, 0)),
                  pl.BlockSpec((1, DN_DV), lambda i, g, t: (0, 0))],
        out_specs=[pl.BlockSpec((1, tt, wv), lambda i, g, t: (i, t, g)),
                   pl.BlockSpec((1, hps, DN_DK, DN_DV), lambda i, g, t: (i, g, 0, 0))],
        out_shape=[jax.ShapeDtypeStruct((b, tp, hh * DN_DV), F32),
                   jax.ShapeDtypeStruct((b, hh, DN_DK, DN_DV), F32)],
        scratch_shapes=[pltpu.VMEM((SUBLANES, wq), F32), pltpu.VMEM((SUBLANES, wq), F32),
                        pltpu.VMEM((SUBLANES, wv), F32), pltpu.VMEM((hps, DN_DK, DN_DV), F32)],
        compiler_params=_cparams(("parallel", "parallel", "arbitrary")),
        name="deltanet_core",
    )(qkv, qkv, qkv, z, ba, buf, buf, buf, s0, conv_w, conv_w, conv_w, avec, dtb, o_gain.reshape(1, DN_DV))


def _topk_mask_rows(g, n_valid, n_rows, k_top):
    jrow = lax.broadcasted_iota(jnp.int32, g.shape, 0)
    cnt = jnp.zeros(g.shape, F32)
    for jp in range(n_rows):
        gb = g[jp:jp + 1, :]
        beats = jnp.where(gb > g, 1.0, jnp.where(gb == g, jnp.where(jrow > jp, 1.0, 0.0), 0.0))
        cnt = cnt + jnp.where(jp < n_valid, beats, 0.0)
    return jnp.where(jrow < n_valid, cnt, float(k_top)) < float(k_top), cnt


LOG2E = 1.4426950408889634
N_EXT = 16


def _split3_f32(a):
    p1 = a.astype(BF16).astype(F32)
    r1 = a - p1
    p2 = r1.astype(BF16).astype(F32)
    return p1, p2, (r1 - p2).astype(BF16).astype(F32)


def _moba_kernel(q_ref, k_ref, v_ref, sl_ref, o_ref, kmean, kaug, vt, pen, sbuf, *, nb, nbp, grp):
    i = pl.program_id(2)
    blk, dh = MOBA_BLOCK, MOBA_DH

    @pl.when(i == 0)
    def _():
        kmean[...] = jnp.zeros_like(kmean)
        lane = lax.broadcasted_iota(jnp.int32, (blk, LANES), 1)
        crow = lax.broadcasted_iota(jnp.int32, (blk, LANES), 0).astype(F32)
        kext = jnp.where(lane < 3, crow, jnp.where(lane < 6, 1.0, 0.0)).astype(BF16)
        for j in range(nb):
            rows = slice(j * blk, (j + 1) * blk)
            kj = k_ref[0, rows, :]
            kmean[j:j + 1, :] = jnp.sum(kj, axis=0, keepdims=True) * (1.0 / blk)
            kaug[j] = jnp.concatenate([kj.astype(BF16), kext], axis=1)
            vt[j] = v_ref[0, rows, :].T.astype(BF16)

    slope = sl_ref[0]
    qt = (q_ref[0] * (dh ** -0.5)).T
    qtb = (qt * LOG2E).astype(BF16)
    gate = _dot3(kmean[...], qt)
    sel, _ = _topk_mask_rows(gate, i, nb, MOBA_TOPK)
    jrow = lax.broadcasted_iota(jnp.int32, (nbp, blk), 0)
    ridx = lax.broadcasted_iota(jnp.int32, (nbp, blk), 1)
    row_j = LOG2E * (jnp.where(sel, 0.0, NEG) - slope * ((i - jrow) * blk + ridx).astype(F32))
    for n, piece in enumerate(_split3_f32(row_j)):
        pen[n] = piece

    slope_pieces = _split3_f32(slope * LOG2E)
    row16 = lax.broadcasted_iota(jnp.int32, (N_EXT, blk), 0)
    zpad = jnp.zeros((LANES - N_EXT, blk), BF16)

    def q_aug(pieces):
        ext = jnp.zeros((N_EXT, blk), F32)
        for n, piece in enumerate(tuple(slope_pieces) + tuple(pieces)):
            ext = jnp.where(row16 == n, piece, ext)
        return jnp.concatenate([qtb, ext.astype(BF16), zpad], axis=0)

    key = lax.broadcasted_iota(jnp.int32, (blk, blk), 0)
    qry = lax.broadcasted_iota(jnp.int32, (blk, blk), 1)
    own_row = -LOG2E * slope * lax.broadcasted_iota(jnp.int32, (1, blk), 1).astype(F32)
    def fold8(x, op):
        return op(x.reshape(blk // SUBLANES, SUBLANES, blk), axis=0)

    s_own = jnp.where(key <= qry, _dot(kaug[i], q_aug(_split3_f32(own_row))), NEG)

    n_trips = (i + grp - 1) // grp

    def pass_scores(g, m8):
        for u in range(grp):
            j = g * grp + u
            s = _dot(kaug[j], q_aug([pen[n, pl.ds(j, 1), :] for n in range(3)]))
            sbuf[j] = s
            m8 = jnp.maximum(m8, fold8(s, jnp.max))
        return m8

    m8 = lax.fori_loop(0, n_trips, pass_scores, fold8(s_own, jnp.max))
    m = jnp.max(m8, axis=0, keepdims=True)

    p = jnp.exp2(s_own - m)
    l8 = fold8(p, jnp.sum)
    acc = _dot(vt[i], p.astype(BF16))

    def pass_values(g, carry):
        l8, acc = carry
        for u in range(grp):
            j = g * grp + u
            p = jnp.exp2(sbuf[j] - m)
            l8 = l8 + fold8(p, jnp.sum)
            acc = acc + _dot(vt[j], p.astype(BF16))
        return l8, acc

    l8, acc = lax.fori_loop(0, n_trips, pass_values, (l8, acc))
    o_ref[0] = (acc / jnp.sum(l8, axis=0, keepdims=True)).T


def _alibi_rows(n_heads, width):
    slopes = jnp.exp2(-8.0 * jnp.arange(1, n_heads + 1, dtype=F32) / n_heads)
    return jnp.broadcast_to(slopes[:, None, None], (n_heads, 1, width))


def moba_prompt_attn(q, k, v):
    b, t, w = q.shape
    hh, dh, blk = MOBA_HEADS, MOBA_DH, MOBA_BLOCK
    assert t % blk == 0
    nb = t // blk
    nbp = -(-nb // SUBLANES) * SUBLANES
    grp = next(g for g in (4, 2, 1) if nb % g == 0)
    kern = functools.partial(_moba_kernel, nb=nb, nbp=nbp, grp=grp)
    return pl.pallas_call(
        kern,
        grid=(b, hh, nb),
        in_specs=[pl.BlockSpec((1, blk, dh), lambda bi, h, i: (bi, i, h)),
                  pl.BlockSpec((1, t, dh), lambda bi, h, i: (bi, 0, h)),
                  pl.BlockSpec((1, t, dh), lambda bi, h, i: (bi, 0, h)),
                  pl.BlockSpec((1, 1, blk), lambda bi, h, i: (h, 0, 0))],
        out_specs=pl.BlockSpec((1, blk, dh), lambda bi, h, i: (bi, i, h)),
        out_shape=jax.ShapeDtypeStruct((b, t, w), F32),
        scratch_shapes=[pltpu.VMEM((nbp, dh), F32), pltpu.VMEM((nb, blk, dh + LANES), BF16),
                        pltpu.VMEM((nb, dh, blk), BF16),
                        pltpu.VMEM((3, nbp, blk), F32), pltpu.VMEM((nb, blk, blk), F32)],
        compiler_params=_cparams(("parallel", "parallel", "arbitrary")),
        name="moba_prompt_attn",
    )(q, k, v, _alibi_rows(hh, blk))


PAGES_PER_STEP = 16
PPB = MOBA_BLOCK // PAGE_SIZE


def _kmean_kernel(pt_ref, *refs):
    page_refs, o_ref = refs[:PAGES_PER_STEP], refs[PAGES_PER_STEP]
    for u in range(0, PAGES_PER_STEP, PPB):
        tot = jnp.sum(page_refs[u][...], axis=0)
        for w in range(1, PPB):
            tot = tot + jnp.sum(page_refs[u + w][...], axis=0)
        o_ref[0, u // PPB] = tot * (1.0 / MOBA_BLOCK)


def moba_block_means(pool_k, layer, page_table):
    n_seq, n_pages = page_table.shape
    _, _, page, hh, dh = pool_k.shape
    assert n_pages % PAGES_PER_STEP == 0 and PAGES_PER_STEP % PPB == 0
    steps = n_pages // PAGES_PER_STEP

    def page_spec(u):
        return pl.BlockSpec((None, None, page, hh, dh),
                            lambda b, g, pt: (layer, pt[b * n_pages + g * PAGES_PER_STEP + u], 0, 0, 0))

    bps = PAGES_PER_STEP // PPB
    return pl.pallas_call(
        _kmean_kernel,
        grid_spec=pltpu.PrefetchScalarGridSpec(
            num_scalar_prefetch=1, grid=(n_seq, steps),
            in_specs=[page_spec(u) for u in range(PAGES_PER_STEP)],
            out_specs=pl.BlockSpec((1, bps, hh, dh), lambda b, g, pt: (b, g, 0, 0))),
        out_shape=jax.ShapeDtypeStruct((n_seq, n_pages // PPB, hh, dh), F32),
        compiler_params=_cparams(("parallel", "arbitrary")),
        name="moba_block_means",
    )(page_table.reshape(-1), *([pool_k] * PAGES_PER_STEP))


def _moba_pick_kernel(q_ref, km_ref, o_ref, *, n_blk, t):
    hh, dh = MOBA_HEADS, MOBA_DH
    q = q_ref[0] * (dh ** -0.5)
    cols = []
    for h in range(hh):
        sl = slice(h * dh, (h + 1) * dh)
        cols.append(_dot3(km_ref[0, :, sl], q[:, sl], nt=True))
    gate = jnp.concatenate(cols, axis=1)
    _, cnt = _topk_mask_rows(gate, n_blk, n_blk, MOBA_TOPK)
    jrow = lax.broadcasted_iota(jnp.int32, gate.shape, 0).astype(F32)
    rows = [jnp.sum(jnp.where(cnt == float(r), jrow, 0.0), axis=0, keepdims=True)
            for r in range(MOBA_TOPK)]
    rows.append(jnp.zeros((SUBLANES - MOBA_TOPK, hh * t), F32))
    o_ref[0] = jnp.concatenate(rows, axis=0).astype(jnp.int32)


def moba_pick_blocks(q, kmean):
    n_seq, t, w = q.shape
    n_blk = kmean.shape[1]
    kern = functools.partial(_moba_pick_kernel, n_blk=n_blk, t=t)
    return pl.pallas_call(
        kern,
        grid=(n_seq,),
        in_specs=[pl.BlockSpec((1, t, w), lambda b: (b, 0, 0)),
                  pl.BlockSpec((1, n_blk, w), lambda b: (b, 0, 0))],
        out_specs=pl.BlockSpec((1, SUBLANES, MOBA_HEADS * t), lambda b: (b, 0, 0)),
        out_shape=jax.ShapeDtypeStruct((n_seq, SUBLANES, MOBA_HEADS * t), jnp.int32),
        compiler_params=_cparams(("parallel",)),
        name="moba_pick_blocks",
    )(q, kmean)


def _moba_dec_kernel(pt_ref, sel_ref, q_ref, kn_ref, vn_ref, sl_ref, pk_hbm, pv_hbm, o_ref,
                     kbuf, vbuf, sem, *, layer, t_len, past, n_pages, n_heads, n_steps):
    n_pg = MOBA_TOPK * PPB
    b, h = pl.program_id(0), pl.program_id(1)
    step = b * n_heads + h
    slot = step % 2
    page = PAGE_SIZE

    def blk_of(bb, hh, t, r):
        return sel_ref[((bb * MOBA_TOPK + r) * n_heads + hh) * t_len + t]

    def copies(bb, hh, sl):
        out = []
        for t in range(t_len):
            for r in range(MOBA_TOPK):
                for u in range(PPB):
                    phys = pt_ref[bb * n_pages + blk_of(bb, hh, t, r) * PPB + u]
                    i = (t * MOBA_TOPK + r) * PPB + u
                    out.append(pltpu.make_async_copy(pk_hbm.at[layer, phys, :, hh, :], kbuf.at[sl, i], sem.at[0, sl]))
                    out.append(pltpu.make_async_copy(pv_hbm.at[layer, phys, :, hh, :], vbuf.at[sl, i], sem.at[1, sl]))
        return out

    @pl.when(step == 0)
    def _():
        for cp in copies(b, h, slot):
            cp.start()

    @pl.when(step + 1 < n_steps)
    def _():
        nxt = step + 1
        for cp in copies(nxt // n_heads, nxt % n_heads, 1 - slot):
            cp.start()

    slope = sl_ref[0][:, :page]
    q = (q_ref[0] * (MOBA_DH ** -0.5)).astype(BF16)
    row = lax.broadcasted_iota(jnp.int32, (t_len, page), 0)
    lane = lax.broadcasted_iota(jnp.int32, (t_len, page), 1)
    s_own = _dot_nt(q, kn_ref[0].astype(BF16))
    qi = lax.broadcasted_iota(jnp.int32, (t_len, t_len), 0)
    ki = lax.broadcasted_iota(jnp.int32, (t_len, t_len), 1)
    s_own = jnp.where(ki <= qi, s_own - slope[:, :t_len] * (qi - ki).astype(F32), NEG)

    for cp in copies(b, h, slot):
        cp.wait()

    scores = [s_own]
    for t in range(t_len):
        for r in range(MOBA_TOPK):
            for u in range(PPB):
                dist = (past + row) - (blk_of(b, h, t, r) * MOBA_BLOCK + u * page + lane)
                s = _dot_nt(q, kbuf[slot, (t * MOBA_TOPK + r) * PPB + u].astype(BF16))
                scores.append(jnp.where(row == t, s - slope * dist.astype(F32), NEG))
    m = scores[0].max(axis=-1, keepdims=True)
    for s in scores[1:]:
        m = jnp.maximum(m, s.max(axis=-1, keepdims=True))
    ps = [jnp.exp(s - m) for s in scores]
    l = ps[0].sum(axis=-1, keepdims=True)
    for p in ps[1:]:
        l = l + p.sum(axis=-1, keepdims=True)
    inv_l = 1.0 / l
    acc = _dot((ps[0] * inv_l).astype(BF16), vn_ref[0].astype(BF16))
    for i in range(t_len * n_pg):
        acc = acc + _dot((ps[1 + i] * inv_l).astype(BF16), vbuf[slot, i].astype(BF16))
    o_ref[0, 0] = acc


def moba_decode_attn(q, k_new, v_new, pool_k, pool_v, layer, page_table, sel):
    n_seq, t_len, w = q.shape
    n_pages = page_table.shape[1]
    hh, dh = MOBA_HEADS, MOBA_DH
    _, _, page, _, _ = pool_k.shape
    past = n_pages * page
    assert past % MOBA_BLOCK == 0
    n_buf = t_len * MOBA_TOPK * PPB
    row = lambda: pl.BlockSpec((1, t_len, dh), lambda b, h, pt, sl: (b, 0, h))
    kern = functools.partial(_moba_dec_kernel, layer=layer, t_len=t_len, past=past, n_pages=n_pages,
                             n_heads=hh, n_steps=n_seq * hh)
    return pl.pallas_call(
        kern,
        grid_spec=pltpu.PrefetchScalarGridSpec(
            num_scalar_prefetch=2, grid=(n_seq, hh),
            in_specs=[row(), row(), row(),
                      pl.BlockSpec((1, 1, MOBA_BLOCK), lambda b, h, pt, sl: (h, 0, 0)),
                      pl.BlockSpec(memory_space=pl.ANY), pl.BlockSpec(memory_space=pl.ANY)],
            out_specs=pl.BlockSpec((1, 1, t_len, dh), lambda b, h, pt, sl: (b, h, 0, 0)),
            scratch_shapes=[pltpu.VMEM((2, n_buf, page, dh), F32), pltpu.VMEM((2, n_buf, page, dh), F32),
                            pltpu.SemaphoreType.DMA((2, 2))]),
        out_shape=jax.ShapeDtypeStruct((n_seq, hh, t_len, dh), F32),
        compiler_params=_cparams(("arbitrary", "arbitrary")),
        name="moba_decode_attn",
    )(page_table.reshape(-1), sel.reshape(-1), q, k_new, v_new, _alibi_rows(hh, MOBA_BLOCK), pool_k, pool_v)


def _last_rows(buf, x, n):
    t = x.shape[1]
    if t >= n:
        return x[:, t - n:]
    return jnp.concatenate([buf[:, buf.shape[1] - (n - t):], x], axis=1)


def _pad_front(buf, rows):
    return jnp.pad(buf, ((0, 0), (rows - buf.shape[1], 0), (0, 0)))


def kernel(x_prompt, x_sample, state_dn_s, state_dn_conv, cache_moba_k, cache_moba_v, page_table, cache_mem_k, cache_mem_v, state_ffn_conv, mem_prompt, norm_mix, norm_xattn, norm_mem, norm_ffn, norm_final, dn_w_in, dn_conv_w, dn_a_log, dn_dt_bias, dn_o_gain, dn_w_out, moba_w_qkv, moba_w_out, xa_w_q, xa_w_kv, xa_w_out, ffn_w_up, ffn_conv_w, ffn_w_down):
    bp, seq, d = x_prompt.shape
    bs, dseq, _ = x_sample.shape
    depth = norm_mix.shape[0]
    mem_len = mem_prompt.shape[1]
    d_ff = ffn_w_down.shape[1]
    qk_w = DN_HEADS * DN_DK
    v_w = DN_HEADS * DN_DV
    conv_ch = 2 * qk_w + v_w
    moba_w = MOBA_HEADS * MOBA_DH
    xa_w = XA_HEADS * XA_DH
    bf = lambda a: a.astype(BF16)

    hp = x_prompt.reshape(bp * seq, d)
    hs = x_sample.reshape(bs * dseq, d)
    outs = {k: [] for k in ("p_dn_s", "p_dn_c", "p_mk", "p_mv", "p_memk", "p_memv", "p_ffc",
                            "s_dn_s", "s_dn_c", "s_mk", "s_mv", "s_ffc")}
    for layer in range(depth):
        if layer % 2 == 0:
            ia = layer // 2
            w_in = dn_w_in[ia]
            ws = [bf(w_in[:, :conv_ch]), bf(w_in[:, conv_ch:conv_ch + v_w]),
                  bf(jnp.pad(w_in[:, conv_ch + v_w:], ((0, 0), (0, LANES - 2 * DN_HEADS))))]
            w_out = bf(dn_w_out[ia])
            args = (dn_conv_w[ia], dn_a_log[ia], dn_dt_bias[ia], dn_o_gain[ia])
            qkv, z, ba = norm_proj(hp, norm_mix[layer], ws, tm=512)
            qkv3 = qkv.reshape(bp, seq, conv_ch)
            o, s_fin = deltanet_core(qkv3, z.reshape(bp, seq, v_w), ba.reshape(bp, seq, LANES),
                                     jnp.zeros((bp, SUBLANES, conv_ch), F32),
                                     jnp.zeros((bp, DN_HEADS, DN_DK, DN_DV), F32), *args,
                                     tt=512, c=DN_CHUNK, t_valid=seq, hps=2)
            hp = out_proj_res(o.reshape(bp * seq, v_w), w_out, hp, tm=512)
            outs["p_dn_s"].append(s_fin)
            outs["p_dn_c"].append(_last_rows(jnp.zeros((bp, DN_CONV - 1, conv_ch), F32), qkv3, DN_CONV - 1))
            qkv, z, ba = norm_proj(hs, norm_mix[layer], ws, tm=256)
            qkv3 = qkv.reshape(bs, dseq, conv_ch)
            tpad = 16
            padt = lambda a: jnp.pad(a.reshape(bs, dseq, -1), ((0, 0), (0, tpad - dseq), (0, 0)))
            o, s_fin = deltanet_core(padt(qkv), padt(z), padt(ba),
                                     _pad_front(state_dn_conv[ia], SUBLANES), state_dn_s[ia], *args,
                                     tt=tpad, c=tpad, t_valid=dseq, hps=DN_HEADS)
            hs = out_proj_res(o[:, :dseq].reshape(bs * dseq, v_w), w_out, hs, tm=256)
            outs["s_dn_s"].append(s_fin)
            outs["s_dn_c"].append(_last_rows(state_dn_conv[ia], qkv3, DN_CONV - 1))
        else:
            ib = layer // 2
            w_qkv = moba_w_qkv[ib]
            ws = [bf(w_qkv[:, j * moba_w:(j + 1) * moba_w]) for j in range(3)]
            w_out = bf(moba_w_out[ib])
            q, k, v = (a.reshape(bp, seq, moba_w) for a in norm_proj(hp, norm_mix[layer], ws, tm=512))
            o = moba_prompt_attn(q, k, v)
            hp = out_proj_res(o.reshape(bp * seq, moba_w), w_out, hp, tm=512)
            outs["p_mk"].append(k.reshape(bp, seq, MOBA_HEADS, MOBA_DH))
            outs["p_mv"].append(v.reshape(bp, seq, MOBA_HEADS, MOBA_DH))
            q, k, v = (a.reshape(bs, dseq, moba_w) for a in norm_proj(hs, norm_mix[layer], ws, tm=256))
            kmean = moba_block_means(cache_moba_k, ib, page_table)
            pick = moba_pick_blocks(q, kmean.reshape(bs, -1, moba_w))
            sel = pick[:, :MOBA_TOPK].reshape(bs, MOBA_TOPK, MOBA_HEADS, dseq)
            o = moba_decode_attn(q, k, v, cache_moba_k, cache_moba_v, ib, page_table, sel)
            o = o.transpose(0, 2, 1, 3).reshape(bs * dseq, moba_w)
            hs = out_proj_res(o, w_out, hs, tm=256)
            outs["s_mk"].append(k.reshape(bs, dseq, MOBA_HEADS, MOBA_DH))
            outs["s_mv"].append(v.reshape(bs, dseq, MOBA_HEADS, MOBA_DH))

        w_kv = xa_w_kv[layer]
        mk, mv = norm_proj(mem_prompt.reshape(bp * mem_len, d), norm_mem[layer],
                           [bf(w_kv[:, :xa_w]), bf(w_kv[:, xa_w:])], tm=256)
        mk = mk.reshape(bp, mem_len, xa_w)
        mv = mv.reshape(bp, mem_len, xa_w)
        outs["p_memk"].append(mk.reshape(bp, mem_len, XA_HEADS, XA_DH))
        outs["p_memv"].append(mv.reshape(bp, mem_len, XA_HEADS, XA_DH))
        w_q, w_o = bf(xa_w_q[layer]), bf(xa_w_out[layer])
        hp = mem_xattn(hp, mk, mv, norm_xattn[layer], w_q, w_o, nb=1, tm=512)
        hs = mem_xattn(hs, cache_mem_k[layer].reshape(bs, mem_len, xa_w),
                       cache_mem_v[layer].reshape(bs, mem_len, xa_w),
                       norm_xattn[layer], w_q, w_o, nb=8, tm=dseq)

        last = layer == depth - 1
        w_up, w_dn = bf(ffn_w_up[layer]), bf(ffn_w_down[layer])
        hp3, tail = conv_ffn(hp.reshape(bp, seq, d), jnp.zeros((bp, SUBLANES, 2 * d_ff), F32),
                             norm_ffn[layer], w_up, ffn_conv_w[layer], w_dn, norm_final,
                             tm=512, stride=1, final_norm=last)
        hp = hp3.reshape(bp * seq, d)
        outs["p_ffc"].append(tail[:, SUBLANES - (FFN_CONV - 1):])
        hs_tm = hs.reshape(bs, dseq, d).transpose(1, 0, 2).reshape(1, dseq * bs, d)
        buf_tm = state_ffn_conv[layer].transpose(1, 0, 2).reshape(1, (FFN_CONV - 1) * bs, 2 * d_ff)
        hs3, tail = conv_ffn(hs_tm, buf_tm, norm_ffn[layer], w_up, ffn_conv_w[layer], w_dn, norm_final,
                             tm=dseq * bs, stride=bs, final_norm=last)
        hs = hs3.reshape(dseq, bs, d).transpose(1, 0, 2).reshape(bs * dseq, d)
        outs["s_ffc"].append(tail.reshape(FFN_CONV - 1, bs, 2 * d_ff).transpose(1, 0, 2))

    y_prompt = hp.reshape(bp, seq, d)
    y_sample = hs.reshape(bs, dseq, d)
    st = lambda k: outs[k][0][None] if len(outs[k]) == 1 else jnp.stack(outs[k])
    return (y_prompt, y_sample, st("p_dn_s"), st("p_dn_c"), st("p_mk"), st("p_mv"),
            st("p_memk"), st("p_memv"), st("p_ffc"),
            st("s_dn_s"), st("s_dn_c"), st("s_mk"), st("s_mv"), st("s_ffc"))
```

```python
import functools

import jax
import jax.numpy as jnp
from jax import lax
from jax.experimental import pallas as pl
from jax.experimental.pallas import tpu as pltpu

F32 = jnp.float32
BF16 = jnp.bfloat16
EPS = 1e-6
NEG = -1e30

DN_HEADS = 8
DN_DK = 128
DN_DV = 128
DN_CONV = 4
DN_CHUNK = 64
MOBA_HEADS = 8
MOBA_DH = 128
MOBA_BLOCK = 256
MOBA_TOPK = 3
PAGE_SIZE = 128
XA_HEADS = 4
XA_DH = 128
FFN_CONV = 3

SUBLANES = 8
LANES = 128
VMEM_LIMIT = 56 * 1024 * 1024


def _cparams(sem):
    return pltpu.CompilerParams(dimension_semantics=sem, vmem_limit_bytes=VMEM_LIMIT)


def _resident(shape):
    nd = len(shape)
    return pl.BlockSpec(shape, lambda *_: (0,) * nd, pipeline_mode=pl.Buffered(1))


def _rms(x, gain):
    return x * lax.rsqrt(jnp.mean(x * x, axis=-1, keepdims=True) + EPS) * gain


def _silu(x):
    return x * jax.nn.sigmoid(x)


def _dot(a, b):
    return jnp.dot(a, b, preferred_element_type=F32)


def _dot_nt(a, b):
    return lax.dot_general(a, b, (((1,), (1,)), ((), ())), preferred_element_type=F32)


def _split2(a):
    hi = a.astype(BF16)
    lo = (a - hi.astype(F32)).astype(BF16)
    return hi, lo


def _dot3(a, b, nt=False):
    d = _dot_nt if nt else _dot
    ah, al = _split2(a)
    bh, bl = _split2(b)
    return d(ah, bh) + (d(al, bh) + d(ah, bl))


def _split3(a):
    p1 = a.astype(BF16)
    r1 = a - p1.astype(F32)
    p2 = r1.astype(BF16)
    r2 = r1 - p2.astype(F32)
    return p1, p2, r2.astype(BF16)


def _dot_sel(sel, x, nt=False):
    d = _dot_nt if nt else _dot
    p1, p2, p3 = _split3(x)
    return d(sel, p1) + (d(sel, p2) + d(sel, p3))


def _norm_proj_kernel(x_ref, g_ref, *refs, n_w):
    w_refs, o_refs = refs[:n_w], refs[n_w:]
    xn = _rms(x_ref[...], g_ref[...]).astype(BF16)
    for w_ref, o_ref in zip(w_refs, o_refs):
        n = w_ref.shape[1]
        for c in range(0, n, 512):
            cw = min(512, n - c)
            o_ref[:, c:c + cw] = _dot(xn, w_ref[:, c:c + cw])


def norm_proj(x, gain, ws, tm):
    r, d = x.shape
    tm = min(tm, r)
    assert r % tm == 0
    n_w = len(ws)
    return pl.pallas_call(
        functools.partial(_norm_proj_kernel, n_w=n_w),
        grid=(r // tm,),
        in_specs=[pl.BlockSpec((tm, d), lambda i: (i, 0)), _resident((1, d))]
        + [_resident(w.shape) for w in ws],
        out_specs=[pl.BlockSpec((tm, w.shape[1]), lambda i: (i, 0)) for w in ws],
        out_shape=[jax.ShapeDtypeStruct((r, w.shape[1]), F32) for w in ws],
        compiler_params=_cparams(("parallel",)),
        name="norm_proj",
    )(x, gain.reshape(1, d), *ws)


def _out_proj_kernel(a_ref, w_ref, h_ref, o_ref):
    o_ref[...] = h_ref[...] + _dot(a_ref[...].astype(BF16), w_ref[...])


def out_proj_res(a, w, h, tm):
    r, k = a.shape
    d = w.shape[1]
    tm = min(tm, r)
    assert r % tm == 0
    return pl.pallas_call(
        _out_proj_kernel,
        grid=(r // tm,),
        in_specs=[pl.BlockSpec((tm, k), lambda i: (i, 0)), _resident(w.shape),
                  pl.BlockSpec((tm, d), lambda i: (i, 0))],
        out_specs=pl.BlockSpec((tm, d), lambda i: (i, 0)),
        out_shape=jax.ShapeDtypeStruct((r, d), F32),
        compiler_params=_cparams(("parallel",)),
        name="out_proj_res",
    )(a, w, h)


def _shift_rows(u, head, stride):
    rows = u.shape[0]
    if stride % SUBLANES == 0:
        return jnp.concatenate([head, u[:rows - stride]], axis=0)
    assert stride == 1
    rolled = pltpu.roll(u, 1, axis=0)
    first = jnp.where(lax.broadcasted_iota(jnp.int32, (SUBLANES, u.shape[1]), 0) == 0, head, rolled[:SUBLANES])
    return jnp.concatenate([first, rolled[SUBLANES:]], axis=0)


def _ffn_kernel(x_ref, buf_ref, g_ref, wup_ref, cw_ref, wdn_ref, fg_ref, o_ref, tail_ref, carry, hid,
                *, tm, stride, pad, d_ff, final_norm, chunk):
    t = pl.program_id(1)

    @pl.when(t == 0)
    def _():
        carry[...] = buf_ref[0]

    x = x_ref[0]
    xn = _rms(x, g_ref[...]).astype(BF16)
    for j0 in range(0, d_ff, chunk):
        cs = []
        for c0 in (j0, d_ff + j0):
            cols = slice(c0, c0 + chunk)
            u = _dot(xn, wup_ref[:, cols])
            prev = carry[:, cols]
            u1 = _shift_rows(u, prev[pad - stride:], stride)
            u2 = _shift_rows(u1, prev[pad - 2 * stride:pad - stride], stride)
            cs.append(u2 * cw_ref[0:1, cols] + u1 * cw_ref[1:2, cols] + u * cw_ref[2:3, cols])
            carry[:, cols] = u[tm - pad:]
        hid[:, j0:j0 + chunk] = (_silu(cs[0]) * cs[1]).astype(BF16)
    out = x + _dot(hid[...], wdn_ref[...])
    if final_norm:
        out = _rms(out, fg_ref[...])
    o_ref[0] = out
    tail_ref[0] = carry[...]


def conv_ffn(h, buf, gain, w_up, conv_w, w_down, final_gain, *, tm, stride, final_norm):
    nb, t, d = h.shape
    pad = buf.shape[1]
    d_ff = w_down.shape[0]
    tm = min(tm, t)
    assert t % tm == 0 and pad % SUBLANES == 0 and pad >= 2 * stride and tm >= pad
    kern = functools.partial(_ffn_kernel, tm=tm, stride=stride, pad=pad, d_ff=d_ff,
                             final_norm=final_norm, chunk=256)
    return pl.pallas_call(
        kern,
        grid=(nb, t // tm),
        in_specs=[pl.BlockSpec((1, tm, d), lambda b, i: (b, i, 0)),
                  pl.BlockSpec((1, pad, 2 * d_ff), lambda b, i: (b, 0, 0)),
                  _resident((1, d)), _resident(w_up.shape), _resident(conv_w.shape),
                  _resident(w_down.shape), _resident((1, d))],
        out_specs=[pl.BlockSpec((1, tm, d), lambda b, i: (b, i, 0)),
                   pl.BlockSpec((1, pad, 2 * d_ff), lambda b, i: (b, 0, 0))],
        out_shape=[jax.ShapeDtypeStruct((nb, t, d), F32),
                   jax.ShapeDtypeStruct((nb, pad, 2 * d_ff), F32)],
        scratch_shapes=[pltpu.VMEM((pad, 2 * d_ff), F32), pltpu.VMEM((tm, d_ff), BF16)],
        compiler_params=_cparams(("parallel", "arbitrary")),
        name="conv_ffn",
    )(h, buf, gain.reshape(1, d), w_up, conv_w, w_down, final_gain.reshape(1, d))


def _xattn_kernel(x_ref, mk_ref, mv_ref, g_ref, wq_ref, wo_ref, o_ref, att, *, nb, tm):
    x = x_ref[...]
    xn = _rms(x, g_ref[...]).astype(BF16)
    q = _dot(xn, wq_ref[...])
    scale = XA_DH ** -0.5
    for b in range(nb):
        rows = slice(b * tm, (b + 1) * tm)
        for hh in range(XA_HEADS):
            cols = slice(hh * XA_DH, (hh + 1) * XA_DH)
            s = _dot_nt(q[rows, cols].astype(BF16), mk_ref[b, :, cols].astype(BF16)) * scale
            p = jnp.exp(s - jnp.max(s, axis=-1, keepdims=True))
            p = p / jnp.sum(p, axis=-1, keepdims=True)
            att[rows, cols] = _dot(p.astype(BF16), mv_ref[b, :, cols].astype(BF16))
    o_ref[...] = x + _dot(att[...].astype(BF16), wo_ref[...])


def mem_xattn(h, mem_k, mem_v, gain, w_q, w_o, *, nb, tm):
    r, d = h.shape
    n_seq, m, w = mem_k.shape
    t = r // n_seq
    assert t % tm == 0 and (nb == 1 or tm == t) and n_seq % nb == 0
    tiles = t // tm
    kern = functools.partial(_xattn_kernel, nb=nb, tm=tm)
    return pl.pallas_call(
        kern,
        grid=(n_seq // nb, tiles),
        in_specs=[pl.BlockSpec((nb * tm, d), lambda b, i: (b * tiles + i, 0)),
                  pl.BlockSpec((nb, m, w), lambda b, i: (b, 0, 0)),
                  pl.BlockSpec((nb, m, w), lambda b, i: (b, 0, 0)),
                  _resident((1, d)), _resident(w_q.shape), _resident(w_o.shape)],
        out_specs=pl.BlockSpec((nb * tm, d), lambda b, i: (b * tiles + i, 0)),
        out_shape=jax.ShapeDtypeStruct((r, d), F32),
        scratch_shapes=[pltpu.VMEM((nb * tm, w), F32)],
        compiler_params=_cparams(("parallel", "arbitrary")),
        name="mem_xattn",
    )(h, mem_k, mem_v, gain.reshape(1, d), w_q, w_o)


def _tri_inv(a, c):
    r = lax.broadcasted_iota(jnp.int32, (c, c), 0)
    q = lax.broadcasted_iota(jnp.int32, (c, c), 1)
    eye = (r == q).astype(F32)
    x = [eye - jnp.where((r >> 1) == (q >> 1), ai, 0.0) for ai in a]
    sh = 1
    while (1 << sh) < c:
        sh += 1
        mask = ((r >> sh) == (q >> sh)) & ((r >> (sh - 1)) != (q >> (sh - 1)))
        xb = [xi.astype(BF16) for xi in x]
        lx = [_dot(jnp.where(mask, ai, 0.0).astype(BF16), xbi) for ai, xbi in zip(a, xb)]
        x = [xi - _dot(xbi, li.astype(BF16)) for xi, xbi, li in zip(x, xb, lx)]
    return x


def _dn_kernel(q_ref, k_ref, v_ref, z_ref, ba_ref, bq_ref, bk_ref, bv_ref, s0_ref,
               cwq_ref, cwk_ref, cwv_ref, avec_ref, dtb_ref, og_ref,
               o_ref, sfin_ref, xq, xk, xv, state, *, tt, c, t_valid, nt, hps):
    hg = pl.program_id(1)
    t = pl.program_id(2)
    lead = SUBLANES

    @pl.when(t == 0)
    def _():
        state[...] = s0_ref[0]
        xq[...] = bq_ref[0]
        xk[...] = bk_ref[0]
        xv[...] = bv_ref[0]

    def conv_silu(hist, x_ref, cw_ref):
        x = x_ref[0]
        taps = [x]
        for j in range(1, DN_CONV):
            taps.append(_shift_rows(taps[-1], hist[lead - j:lead - j + 1], 1))
        y = taps[DN_CONV - 1] * cw_ref[0:1]
        for j in range(1, DN_CONV):
            y = y + taps[DN_CONV - 1 - j] * cw_ref[j:j + 1]
        hist[...] = x[tt - lead:]
        return _silu(y)

    qa = conv_silu(xq, q_ref, cwq_ref)
    ka = conv_silu(xk, k_ref, cwk_ref)
    va = conv_silu(xv, v_ref, cwv_ref)

    ba = ba_ref[0]
    lane = lax.broadcasted_iota(jnp.int32, (tt, LANES), 1)
    beta_all = jax.nn.sigmoid(ba)
    g_all = -avec_ref[...] * jax.nn.softplus(ba + dtb_ref[...])
    if t_valid < nt * tt:
        row = t * tt + lax.broadcasted_iota(jnp.int32, (tt, LANES), 0)
        beta_all = jnp.where(row < t_valid, beta_all, 0.0)
        g_all = jnp.where(row < t_valid, g_all, 0.0)

    ri = lax.broadcasted_iota(jnp.int32, (c, c), 0)
    ci = lax.broadcasted_iota(jnp.int32, (c, c), 1)
    causal = ri >= ci
    strict = ri > ci
    ltri = causal.astype(BF16)
    ones8 = jnp.ones((SUBLANES, LANES), BF16)
    og = og_ref[...]

    q, k, v, beta, g_sel = [], [], [], [], []
    for hh in range(hps):
        cols = slice(hh * DN_DK, (hh + 1) * DN_DK)
        head = hg * hps + hh
        qh, kh = qa[:, cols], ka[:, cols]
        q.append(qh * lax.rsqrt(jnp.sum(qh * qh, axis=-1, keepdims=True) + EPS) * (DN_DK ** -0.5))
        k.append(kh * lax.rsqrt(jnp.sum(kh * kh, axis=-1, keepdims=True) + EPS))
        v.append(va[:, hh * DN_DV:(hh + 1) * DN_DV])
        beta.append(jnp.sum(jnp.where(lane == head, beta_all, 0.0), axis=1, keepdims=True))
        g_sel.append(jnp.where(lane == head + DN_HEADS, g_all, 0.0))

    n_ch = tt // c
    items = [(hh, slice(ch * c, (ch + 1) * c)) for ch in range(n_ch) for hh in range(hps)]
    gcs = [_dot_sel(ltri, g_sel[hh][sl]) for hh, sl in items]
    gcols = [jnp.sum(gc, axis=1, keepdims=True) for gc in gcs]
    grows = [_dot_sel(ones8, gc, nt=True)[0:1] for gc in gcs]
    egs = [jnp.exp(gcol) for gcol in gcols]
    kbs = [k[hh][sl] * beta[hh][sl] for hh, sl in items]
    kqs = [_dot_nt(jnp.concatenate([kb, q[hh][sl]], axis=0).astype(BF16), k[hh][sl].astype(BF16))
           for kb, (hh, sl) in zip(kbs, items)]
    gams = [jnp.exp(jnp.where(causal, gcol - grow, NEG)) for gcol, grow in zip(gcols, grows)]
    t_invs = _tri_inv([jnp.where(strict, kq[:c] * gam, 0.0) for kq, gam in zip(kqs, gams)], c)
    uws = [_dot(ti.astype(BF16),
                jnp.concatenate([v[hh][sl] * beta[hh][sl], kb * eg], axis=1).astype(BF16)).astype(BF16)
           for ti, (hh, sl), kb, eg in zip(t_invs, items, kbs, egs)]
    qkuws = [_dot((kq[c:] * gam).astype(BF16), uw) for kq, gam, uw in zip(kqs, gams, uws)]
    glasts = [gcol[c - 1:c] for gcol in gcols]
    kduws = [lax.dot_general((k[hh][sl] * jnp.exp(glast - gcol)).astype(BF16), uw, (((0,), (0,)), ((), ())),
                             preferred_element_type=F32)
             for (hh, sl), glast, gcol, uw in zip(items, glasts, gcols, uws)]
    lhss = [jnp.concatenate([q[hh][sl] * eg - qkuw[:, DN_DV:], kduw[:, DN_DV:]], axis=0).astype(BF16)
            for (hh, sl), eg, qkuw, kduw in zip(items, egs, qkuws, kduws)]

    s = [state[hh] for hh in range(hps)]
    for (hh, sl), lhs, qkuw, kduw, glast in zip(items, lhss, qkuws, kduws, glasts):
        r = _dot(lhs, s[hh].astype(BF16))
        o = r[:c] + qkuw[:, :DN_DV]
        s[hh] = s[hh] * jnp.exp(glast) - r[c:] + kduw[:, :DN_DV]
        o = o * lax.rsqrt(jnp.mean(o * o, axis=-1, keepdims=True) + EPS) * og
        cols = slice(hh * DN_DV, (hh + 1) * DN_DV)
        o_ref[0, sl, cols] = o * _silu(z_ref[0, sl, cols])
    for hh in range(hps):
        state[hh] = s[hh]

    @pl.when(t == nt - 1)
    def _():
        sfin_ref[0] = state[...]


def deltanet_core(qkv, z, ba, buf, s0, conv_w, a_log, dt_bias, o_gain, *, tt, c, t_valid, hps):
    b, tp, _ = qkv.shape
    hh = DN_HEADS
    nt = tp // tt
    assert tp % tt == 0 and tt % c == 0 and hh % hps == 0
    ng = hh // hps
    zpad = jnp.zeros((LANES - 2 * hh,), F32)
    avec = jnp.concatenate([jnp.zeros((hh,), F32), jnp.exp(a_log), zpad]).reshape(1, LANES)
    dtb = jnp.concatenate([jnp.zeros((hh,), F32), dt_bias, zpad]).reshape(1, LANES)
    wq, wv = hps * DN_DK, hps * DN_DV
    col = lambda part: pl.BlockSpec((1, tt, wq), lambda i, g, t: (i, t, part * ng + g))
    bufc = lambda part: pl.BlockSpec((1, SUBLANES, wq), lambda i, g, t: (i, 0, part * ng + g))
    cwc = lambda part: pl.BlockSpec((DN_CONV, wq), lambda i, g, t: (0, part * ng + g))
    kern = functools.partial(_dn_kernel, tt=tt, c=c, t_valid=t_valid, nt=nt, hps=hps)
    return pl.pallas_call(
        kern,
        grid=(b, ng, nt),
        in_specs=[col(0), col(1), col(2),
                  pl.BlockSpec((1, tt, wv), lambda i, g, t: (i, t, g)),
                  pl.BlockSpec((1, tt, LANES), lambda i, g, t: (i, t, 0)),
                  bufc(0), bufc(1), bufc(2),
                  pl.BlockSpec((1, hps, DN_DK, DN_DV), lambda i, g, t: (i, g, 0, 0)),
                  cwc(0), cwc(1), cwc(2),
                  pl.BlockSpec((1, LANES), lambda i, g, t: (0, 0)),
                  pl.BlockSpec((1, LANES), lambda i, g, t: (0, 0)),
                  pl.BlockSpec((1, DN_DV), lambda i, g, t: (0, 0))],
        out_specs=[pl.BlockSpec((1, tt, wv), lambda i, g, t: (i, t, g)),
                   pl.BlockSpec((1, hps, DN_DK, DN_DV), lambda i, g, t: (i, g, 0, 0))],
        out_shape=[jax.ShapeDtypeStruct((b, tp, hh * DN_DV), F32),
                   jax.ShapeDtypeStruct((b, hh, DN_DK, DN_DV), F32)],
        scratch_shapes=[pltpu.VMEM((SUBLANES, wq), F32), pltpu.VMEM((SUBLANES, wq), F32),
                        pltpu.VMEM((SUBLANES, wv), F32), pltpu.VMEM((hps, DN_DK, DN_DV), F32)],
        compiler_params=_cparams(("parallel", "parallel", "arbitrary")),
        name="deltanet_core",
    )(qkv, qkv, qkv, z, ba, buf, buf, buf, s0, conv_w, conv_w, conv_w, avec, dtb, o_gain.reshape(1, DN_DV))


def _topk_mask_rows(g, n_valid, n_rows, k_top):
    jrow = lax.broadcasted_iota(jnp.int32, g.shape, 0)
    cnt = jnp.zeros(g.shape, F32)
    for jp in range(min(n_rows, n_valid)):
        gb = g[jp:jp + 1, :]
        cnt = cnt + jnp.where(gb > g, 1.0, jnp.where(gb == g, jnp.where(jrow > jp, 1.0, 0.0), 0.0))
    return jnp.where(jrow < n_valid, cnt, float(k_top)) < float(k_top), cnt


LOG2E = 1.4426950408889634
N_EXT = 16


def _split3_f32(a):
    p1 = a.astype(BF16).astype(F32)
    r1 = a - p1
    p2 = r1.astype(BF16).astype(F32)
    return p1, p2, (r1 - p2).astype(BF16).astype(F32)


def _moba_kernel(q_ref, k_ref, v_ref, sl_ref, o_ref, kmean, kaug, vt, qtb_s, pen, sbuf, *, nb, nbp, grp):
    i = pl.program_id(2)
    blk, dh = MOBA_BLOCK, MOBA_DH
    slope = sl_ref[0]

    @pl.when(i == 0)
    def _():
        kmean[...] = jnp.zeros_like(kmean)
        lane = lax.broadcasted_iota(jnp.int32, (blk, LANES), 1)
        crow = lax.broadcasted_iota(jnp.int32, (blk, LANES), 0).astype(F32)
        kext = jnp.where(lane < 3, crow, jnp.where(lane < 6, 1.0, 0.0)).astype(BF16)
        for j in range(nb):
            rows = slice(j * blk, (j + 1) * blk)
            kj = k_ref[0, rows, :]
            kmean[j:j + 1, :] = jnp.sum(kj, axis=0, keepdims=True) * (1.0 / blk)
            kaug[j] = jnp.concatenate([kj.astype(BF16), kext], axis=1)
            vt[j] = v_ref[0, rows, :].T.astype(BF16)
        km = kmean[...]
        jrow = lax.broadcasted_iota(jnp.int32, (nbp, blk), 0)
        ridx = lax.broadcasted_iota(jnp.int32, (nbp, blk), 1)
        for jq in range(nb):
            qt = (q_ref[0, jq * blk:(jq + 1) * blk, :] * (dh ** -0.5)).T
            qtb_s[jq] = (qt * LOG2E).astype(BF16)
            sel, _ = _topk_mask_rows(_dot3(km, qt), jq, nb, MOBA_TOPK)
            add = jnp.where(jrow == jq, 0.0, jnp.where(sel, 0.0, NEG))
            row_j = LOG2E * (add - slope * ((jq - jrow) * blk + ridx).astype(F32))
            for n, piece in enumerate(_split3_f32(row_j)):
                pen[n, jq] = piece

    qtb = qtb_s[i]
    slope_pieces = _split3_f32(slope * LOG2E)
    row16 = lax.broadcasted_iota(jnp.int32, (N_EXT, blk), 0)
    zpad = jnp.zeros((LANES - N_EXT, blk), BF16)

    def q_aug(pieces):
        ext = jnp.zeros((N_EXT, blk), F32)
        for n, piece in enumerate(tuple(slope_pieces) + tuple(pieces)):
            ext = jnp.where(row16 == n, piece, ext)
        return jnp.concatenate([qtb, ext.astype(BF16), zpad], axis=0)

    key_minus_qry = (lax.broadcasted_iota(jnp.int32, (blk, blk), 0)
                     - lax.broadcasted_iota(jnp.int32, (blk, blk), 1))

    def fold8(x, op):
        return op(x.reshape(blk // SUBLANES, SUBLANES, blk), axis=0)

    n_trips = i // grp + 1

    def pass_scores(g, m8):
        for u in range(grp):
            j = g * grp + u
            s = _dot(kaug[j], q_aug([pen[n, i, pl.ds(j, 1), :] for n in range(3)]))
            s = jnp.where(key_minus_qry > jnp.where(j == i, 0, blk), NEG, s)
            sbuf[j] = s
            m8 = jnp.maximum(m8, fold8(s, jnp.max))
        return m8

    m8 = lax.fori_loop(0, n_trips, pass_scores, jnp.full((SUBLANES, blk), NEG, F32))
    m = jnp.max(m8, axis=0, keepdims=True)

    def pass_values(g, carry):
        l8, acc = carry
        for u in range(grp):
            j = g * grp + u
            p = jnp.exp2(sbuf[j] - m)
            l8 = l8 + fold8(p, jnp.sum)
            acc = acc + _dot(vt[j], p.astype(BF16))
        return l8, acc

    l8, acc = lax.fori_loop(0, n_trips, pass_values,
                            (jnp.zeros((SUBLANES, blk), F32), jnp.zeros((dh, blk), F32)))
    o_ref[0] = (acc / jnp.sum(l8, axis=0, keepdims=True)).T


def _alibi_rows(n_heads, width):
    slopes = jnp.exp2(-8.0 * jnp.arange(1, n_heads + 1, dtype=F32) / n_heads)
    return jnp.broadcast_to(slopes[:, None, None], (n_heads, 1, width))


def moba_prompt_attn(q, k, v):
    b, t, w = q.shape
    hh, dh, blk = MOBA_HEADS, MOBA_DH, MOBA_BLOCK
    assert t % blk == 0
    nb = t // blk
    nbp = -(-nb // SUBLANES) * SUBLANES
    grp = next(g for g in (4, 2, 1) if nb % g == 0)
    kern = functools.partial(_moba_kernel, nb=nb, nbp=nbp, grp=grp)
    return pl.pallas_call(
        kern,
        grid=(b, hh, nb),
        in_specs=[pl.BlockSpec((1, t, dh), lambda bi, h, i: (bi, 0, h)),
                  pl.BlockSpec((1, t, dh), lambda bi, h, i: (bi, 0, h)),
                  pl.BlockSpec((1, t, dh), lambda bi, h, i: (bi, 0, h)),
                  pl.BlockSpec((1, 1, blk), lambda bi, h, i: (h, 0, 0))],
        out_specs=pl.BlockSpec((1, blk, dh), lambda bi, h, i: (bi, i, h)),
        out_shape=jax.ShapeDtypeStruct((b, t, w), F32),
        scratch_shapes=[pltpu.VMEM((nbp, dh), F32), pltpu.VMEM((nb, blk, dh + LANES), BF16),
                        pltpu.VMEM((nb, dh, blk), BF16), pltpu.VMEM((nb, dh, blk), BF16),
                        pltpu.VMEM((3, nb, nbp, blk), F32), pltpu.VMEM((nb, blk, blk), F32)],
        compiler_params=_cparams(("parallel", "parallel", "arbitrary")),
        name="moba_prompt_attn",
    )(q, k, v, _alibi_rows(hh, blk))


PAGES_PER_STEP = 16
PPB = MOBA_BLOCK // PAGE_SIZE


def _kmean_kernel(pt_ref, *refs):
    page_refs, o_ref = refs[:PAGES_PER_STEP], refs[PAGES_PER_STEP]
    for u in range(0, PAGES_PER_STEP, PPB):
        tot = jnp.sum(page_refs[u][...], axis=0)
        for w in range(1, PPB):
            tot = tot + jnp.sum(page_refs[u + w][...], axis=0)
        o_ref[0, u // PPB] = tot * (1.0 / MOBA_BLOCK)


def moba_block_means(pool_k, layer, page_table):
    n_seq, n_pages = page_table.shape
    _, _, page, hh, dh = pool_k.shape
    assert n_pages % PAGES_PER_STEP == 0 and PAGES_PER_STEP % PPB == 0
    steps = n_pages // PAGES_PER_STEP

    def page_spec(u):
        return pl.BlockSpec((None, None, page, hh, dh),
                            lambda b, g, pt: (layer, pt[b * n_pages + g * PAGES_PER_STEP + u], 0, 0, 0))

    bps = PAGES_PER_STEP // PPB
    return pl.pallas_call(
        _kmean_kernel,
        grid_spec=pltpu.PrefetchScalarGridSpec(
            num_scalar_prefetch=1, grid=(n_seq, steps),
            in_specs=[page_spec(u) for u in range(PAGES_PER_STEP)],
            out_specs=pl.BlockSpec((1, bps, hh, dh), lambda b, g, pt: (b, g, 0, 0))),
        out_shape=jax.ShapeDtypeStruct((n_seq, n_pages // PPB, hh, dh), F32),
        compiler_params=_cparams(("parallel", "arbitrary")),
        name="moba_block_means",
    )(page_table.reshape(-1), *([pool_k] * PAGES_PER_STEP))


def _moba_pick_kernel(q_ref, km_ref, o_ref, *, n_blk, t):
    hh, dh = MOBA_HEADS, MOBA_DH
    q = q_ref[0] * (dh ** -0.5)
    cols = []
    for h in range(hh):
        sl = slice(h * dh, (h + 1) * dh)
        cols.append(_dot3(km_ref[0, :, sl], q[:, sl], nt=True))
    gate = jnp.concatenate(cols, axis=1)
    _, cnt = _topk_mask_rows(gate, n_blk, n_blk, MOBA_TOPK)
    jrow = lax.broadcasted_iota(jnp.int32, gate.shape, 0).astype(F32)
    rows = [jnp.sum(jnp.where(cnt == float(r), jrow, 0.0), axis=0, keepdims=True)
            for r in range(MOBA_TOPK)]
    rows.append(jnp.zeros((SUBLANES - MOBA_TOPK, hh * t), F32))
    o_ref[0] = jnp.concatenate(rows, axis=0).astype(jnp.int32)


def moba_pick_blocks(q, kmean):
    n_seq, t, w = q.shape
    n_blk = kmean.shape[1]
    kern = functools.partial(_moba_pick_kernel, n_blk=n_blk, t=t)
    return pl.pallas_call(
        kern,
        grid=(n_seq,),
        in_specs=[pl.BlockSpec((1, t, w), lambda b: (b, 0, 0)),
                  pl.BlockSpec((1, n_blk, w), lambda b: (b, 0, 0))],
        out_specs=pl.BlockSpec((1, SUBLANES, MOBA_HEADS * t), lambda b: (b, 0, 0)),
        out_shape=jax.ShapeDtypeStruct((n_seq, SUBLANES, MOBA_HEADS * t), jnp.int32),
        compiler_params=_cparams(("parallel",)),
        name="moba_pick_blocks",
    )(q, kmean)


def _moba_dec_kernel(pt_ref, sel_ref, q_ref, kn_ref, vn_ref, sl_ref, pk_hbm, pv_hbm, o_ref,
                     kbuf, vbuf, sem, *, layer, t_len, past, n_pages, n_heads, n_steps):
    n_pg = MOBA_TOPK * PPB
    b, h = pl.program_id(0), pl.program_id(1)
    step = b * n_heads + h
    slot = step % 2
    page = PAGE_SIZE

    def blk_of(bb, hh, t, r):
        return sel_ref[((bb * MOBA_TOPK + r) * n_heads + hh) * t_len + t]

    def copies(bb, hh, sl):
        out = []
        for t in range(t_len):
            for r in range(MOBA_TOPK):
                for u in range(PPB):
                    phys = pt_ref[bb * n_pages + blk_of(bb, hh, t, r) * PPB + u]
                    i = (t * MOBA_TOPK + r) * PPB + u
                    out.append(pltpu.make_async_copy(pk_hbm.at[layer, phys, :, hh, :], kbuf.at[sl, i], sem.at[0, sl]))
                    out.append(pltpu.make_async_copy(pv_hbm.at[layer, phys, :, hh, :], vbuf.at[sl, i], sem.at[1, sl]))
        return out

    @pl.when(step == 0)
    def _():
        for cp in copies(b, h, slot):
            cp.start()

    @pl.when(step + 1 < n_steps)
    def _():
        nxt = step + 1
        for cp in copies(nxt // n_heads, nxt % n_heads, 1 - slot):
            cp.start()

    slope = sl_ref[0][:, :page]
    q = (q_ref[0] * (MOBA_DH ** -0.5)).astype(BF16)
    row = lax.broadcasted_iota(jnp.int32, (t_len, page), 0)
    lane = lax.broadcasted_iota(jnp.int32, (t_len, page), 1)
    s_own = _dot_nt(q, kn_ref[0].astype(BF16))
    qi = lax.broadcasted_iota(jnp.int32, (t_len, t_len), 0)
    ki = lax.broadcasted_iota(jnp.int32, (t_len, t_len), 1)
    s_own = jnp.where(ki <= qi, s_own - slope[:, :t_len] * (qi - ki).astype(F32), NEG)

    for cp in copies(b, h, slot):
        cp.wait()

    scores = [s_own]
    for t in range(t_len):
        for r in range(MOBA_TOPK):
            for u in range(PPB):
                dist = (past + row) - (blk_of(b, h, t, r) * MOBA_BLOCK + u * page + lane)
                s = _dot_nt(q, kbuf[slot, (t * MOBA_TOPK + r) * PPB + u].astype(BF16))
                scores.append(jnp.where(row == t, s - slope * dist.astype(F32), NEG))
    def tree(xs, op):
        while len(xs) > 1:
            xs = [op(xs[n], xs[n + 1]) for n in range(0, len(xs) - 1, 2)] + ([xs[-1]] if len(xs) % 2 else [])
        return xs[0]

    m = jnp.maximum(scores[0].max(axis=-1, keepdims=True),
                    tree(scores[1:], jnp.maximum).max(axis=-1, keepdims=True))
    ps = [jnp.exp(s - m) for s in scores]
    l = ps[0].sum(axis=-1, keepdims=True) + tree(ps[1:], jnp.add).sum(axis=-1, keepdims=True)
    inv_l = 1.0 / l
    acc = _dot((ps[0] * inv_l).astype(BF16), vn_ref[0].astype(BF16))
    for i in range(t_len * n_pg):
        acc = acc + _dot((ps[1 + i] * inv_l).astype(BF16), vbuf[slot, i].astype(BF16))
    o_ref[0, 0] = acc


def moba_decode_attn(q, k_new, v_new, pool_k, pool_v, layer, page_table, sel):
    n_seq, t_len, w = q.shape
    n_pages = page_table.shape[1]
    hh, dh = MOBA_HEADS, MOBA_DH
    _, _, page, _, _ = pool_k.shape
    past = n_pages * page
    assert past % MOBA_BLOCK == 0
    n_buf = t_len * MOBA_TOPK * PPB
    row = lambda: pl.BlockSpec((1, t_len, dh), lambda b, h, pt, sl: (b, 0, h))
    kern = functools.partial(_moba_dec_kernel, layer=layer, t_len=t_len, past=past, n_pages=n_pages,
                             n_heads=hh, n_steps=n_seq * hh)
    return pl.pallas_call(
        kern,
        grid_spec=pltpu.PrefetchScalarGridSpec(
            num_scalar_prefetch=2, grid=(n_seq, hh),
            in_specs=[row(), row(), row(),
                      pl.BlockSpec((1, 1, MOBA_BLOCK), lambda b, h, pt, sl: (h, 0, 0)),
                      pl.BlockSpec(memory_space=pl.ANY), pl.BlockSpec(memory_space=pl.ANY)],
            out_specs=pl.BlockSpec((1, 1, t_len, dh), lambda b, h, pt, sl: (b, h, 0, 0)),
            scratch_shapes=[pltpu.VMEM((2, n_buf, page, dh), F32), pltpu.VMEM((2, n_buf, page, dh), F32),
                            pltpu.SemaphoreType.DMA((2, 2))]),
        out_shape=jax.ShapeDtypeStruct((n_seq, hh, t_len, dh), F32),
        compiler_params=_cparams(("arbitrary", "arbitrary")),
        name="moba_decode_attn",
    )(page_table.reshape(-1), sel.reshape(-1), q, k_new, v_new, _alibi_rows(hh, MOBA_BLOCK), pool_k, pool_v)


def _last_rows(buf, x, n):
    t = x.shape[1]
    if t >= n:
        return x[:, t - n:]
    return jnp.concatenate([buf[:, buf.shape[1] - (n - t):], x], axis=1)


def _pad_front(buf, rows):
    return jnp.pad(buf, ((0, 0), (rows - buf.shape[1], 0), (0, 0)))


def kernel(x_prompt, x_sample, state_dn_s, state_dn_conv, cache_moba_k, cache_moba_v, page_table, cache_mem_k, cache_mem_v, state_ffn_conv, mem_prompt, norm_mix, norm_xattn, norm_mem, norm_ffn, norm_final, dn_w_in, dn_conv_w, dn_a_log, dn_dt_bias, dn_o_gain, dn_w_out, moba_w_qkv, moba_w_out, xa_w_q, xa_w_kv, xa_w_out, ffn_w_up, ffn_conv_w, ffn_w_down):
    bp, seq, d = x_prompt.shape
    bs, dseq, _ = x_sample.shape
    depth = norm_mix.shape[0]
    mem_len = mem_prompt.shape[1]
    d_ff = ffn_w_down.shape[1]
    qk_w = DN_HEADS * DN_DK
    v_w = DN_HEADS * DN_DV
    conv_ch = 2 * qk_w + v_w
    moba_w = MOBA_HEADS * MOBA_DH
    xa_w = XA_HEADS * XA_DH
    bf = lambda a: a.astype(BF16)

    hp = x_prompt.reshape(bp * seq, d)
    hs = x_sample.reshape(bs * dseq, d)
    outs = {k: [] for k in ("p_dn_s", "p_dn_c", "p_mk", "p_mv", "p_memk", "p_memv", "p_ffc",
                            "s_dn_s", "s_dn_c", "s_mk", "s_mv", "s_ffc")}
    for layer in range(depth):
        if layer % 2 == 0:
            ia = layer // 2
            w_in = dn_w_in[ia]
            ws = [bf(w_in[:, :conv_ch]), bf(w_in[:, conv_ch:conv_ch + v_w]),
                  bf(jnp.pad(w_in[:, conv_ch + v_w:], ((0, 0), (0, LANES - 2 * DN_HEADS))))]
            w_out = bf(dn_w_out[ia])
            args = (dn_conv_w[ia], dn_a_log[ia], dn_dt_bias[ia], dn_o_gain[ia])
            qkv, z, ba = norm_proj(hp, norm_mix[layer], ws, tm=512)
            qkv3 = qkv.reshape(bp, seq, conv_ch)
            o, s_fin = deltanet_core(qkv3, z.reshape(bp, seq, v_w), ba.reshape(bp, seq, LANES),
                                     jnp.zeros((bp, SUBLANES, conv_ch), F32),
                                     jnp.zeros((bp, DN_HEADS, DN_DK, DN_DV), F32), *args,
                                     tt=512, c=DN_CHUNK, t_valid=seq, hps=2)
            hp = out_proj_res(o.reshape(bp * seq, v_w), w_out, hp, tm=512)
            outs["p_dn_s"].append(s_fin)
            outs["p_dn_c"].append(_last_rows(jnp.zeros((bp, DN_CONV - 1, conv_ch), F32), qkv3, DN_CONV - 1))
            qkv, z, ba = norm_proj(hs, norm_mix[layer], ws, tm=256)
            qkv3 = qkv.reshape(bs, dseq, conv_ch)
            tpad = 16
            padt = lambda a: jnp.pad(a.reshape(bs, dseq, -1), ((0, 0), (0, tpad - dseq), (0, 0)))
            o, s_fin = deltanet_core(padt(qkv), padt(z), padt(ba),
                                     _pad_front(state_dn_conv[ia], SUBLANES), state_dn_s[ia], *args,
                                     tt=tpad, c=tpad, t_valid=dseq, hps=DN_HEADS)
            hs = out_proj_res(o[:, :dseq].reshape(bs * dseq, v_w), w_out, hs, tm=256)
            outs["s_dn_s"].append(s_fin)
            outs["s_dn_c"].append(_last_rows(state_dn_conv[ia], qkv3, DN_CONV - 1))
        else:
            ib = layer // 2
            w_qkv = moba_w_qkv[ib]
            ws = [bf(w_qkv[:, j * moba_w:(j + 1) * moba_w]) for j in range(3)]
            w_out = bf(moba_w_out[ib])
            q, k, v = (a.reshape(bp, seq, moba_w) for a in norm_proj(hp, norm_mix[layer], ws, tm=512))
            o = moba_prompt_attn(q, k, v)
            hp = out_proj_res(o.reshape(bp * seq, moba_w), w_out, hp, tm=512)
            outs["p_mk"].append(k.reshape(bp, seq, MOBA_HEADS, MOBA_DH))
            outs["p_mv"].append(v.reshape(bp, seq, MOBA_HEADS, MOBA_DH))
            q, k, v = (a.reshape(bs, dseq, moba_w) for a in norm_proj(hs, norm_mix[layer], ws, tm=256))
            kmean = moba_block_means(cache_moba_k, ib, page_table)
            pick = moba_pick_blocks(q, kmean.reshape(bs, -1, moba_w))
            sel = pick[:, :MOBA_TOPK].reshape(bs, MOBA_TOPK, MOBA_HEADS, dseq)
            o = moba_decode_attn(q, k, v, cache_moba_k, cache_moba_v, ib, page_table, sel)
            o = o.transpose(0, 2, 1, 3).reshape(bs * dseq, moba_w)
            hs = out_proj_res(o, w_out, hs, tm=256)
            outs["s_mk"].append(k.reshape(bs, dseq, MOBA_HEADS, MOBA_DH))
            outs["s_mv"].append(v.reshape(bs, dseq, MOBA_HEADS, MOBA_DH))

        w_kv = xa_w_kv[layer]
        mk, mv = norm_proj(mem_prompt.reshape(bp * mem_len, d), norm_mem[layer],
                           [bf(w_kv[:, :xa_w]), bf(w_kv[:, xa_w:])], tm=256)
        mk = mk.reshape(bp, mem_len, xa_w)
        mv = mv.reshape(bp, mem_len, xa_w)
        outs["p_memk"].append(mk.reshape(bp, mem_len, XA_HEADS, XA_DH))
        outs["p_memv"].append(mv.reshape(bp, mem_len, XA_HEADS, XA_DH))
        w_q, w_o = bf(xa_w_q[layer]), bf(xa_w_out[layer])
        hp = mem_xattn(hp, mk, mv, norm_xattn[layer], w_q, w_o, nb=1, tm=512)
        hs = mem_xattn(hs, cache_mem_k[layer].reshape(bs, mem_len, xa_w),
                       cache_mem_v[layer].reshape(bs, mem_len, xa_w),
                       norm_xattn[layer], w_q, w_o, nb=8, tm=dseq)

        last = layer == depth - 1
        w_up, w_dn = bf(ffn_w_up[layer]), bf(ffn_w_down[layer])
        hp3, tail = conv_ffn(hp.reshape(bp, seq, d), jnp.zeros((bp, SUBLANES, 2 * d_ff), F32),
                             norm_ffn[layer], w_up, ffn_conv_w[layer], w_dn, norm_final,
                             tm=512, stride=1, final_norm=last)
        hp = hp3.reshape(bp * seq, d)
        outs["p_ffc"].append(tail[:, SUBLANES - (FFN_CONV - 1):])
        hs_tm = hs.reshape(bs, dseq, d).transpose(1, 0, 2).reshape(1, dseq * bs, d)
        buf_tm = state_ffn_conv[layer].transpose(1, 0, 2).reshape(1, (FFN_CONV - 1) * bs, 2 * d_ff)
        hs3, tail = conv_ffn(hs_tm, buf_tm, norm_ffn[layer], w_up, ffn_conv_w[layer], w_dn, norm_final,
                             tm=dseq * bs, stride=bs, final_norm=last)
        hs = hs3.reshape(dseq, bs, d).transpose(1, 0, 2).reshape(bs * dseq, d)
        outs["s_ffc"].append(tail.reshape(FFN_CONV - 1, bs, 2 * d_ff).transpose(1, 0, 2))

    y_prompt = hp.reshape(bp, seq, d)
    y_sample = hs.reshape(bs, dseq, d)
    st = lambda k: outs[k][0][None] if len(outs[k]) == 1 else jnp.stack(outs[k])
    return (y_prompt, y_sample, st("p_dn_s"), st("p_dn_c"), st("p_mk"), st("p_mv"),
            st("p_memk"), st("p_memv"), st("p_ffc"),
            st("s_dn_s"), st("s_dn_c"), st("s_mk"), st("s_mv"), st("s_ffc"))
```

```python
import functools

import jax
import jax.numpy as jnp
from jax import lax
from jax.experimental import pallas as pl
from jax.experimental.pallas import tpu as pltpu

F32 = jnp.float32
BF16 = jnp.bfloat16
EPS = 1e-6
NEG = -1e30

DN_HEADS = 8
DN_DK = 128
DN_DV = 128
DN_CONV = 4
DN_CHUNK = 64
MOBA_HEADS = 8
MOBA_DH = 128
MOBA_BLOCK = 256
MOBA_TOPK = 3
PAGE_SIZE = 128
XA_HEADS = 4
XA_DH = 128
FFN_CONV = 3

SUBLANES = 8
LANES = 128
VMEM_LIMIT = 56 * 1024 * 1024


def _cparams(sem):
    return pltpu.CompilerParams(dimension_semantics=sem, vmem_limit_bytes=VMEM_LIMIT)


def _resident(shape):
    nd = len(shape)
    return pl.BlockSpec(shape, lambda *_: (0,) * nd, pipeline_mode=pl.Buffered(1))


def _rms(x, gain):
    return x * lax.rsqrt(jnp.mean(x * x, axis=-1, keepdims=True) + EPS) * gain


def _silu(x):
    return x * jax.nn.sigmoid(x)


def _dot(a, b):
    return jnp.dot(a, b, preferred_element_type=F32)


def _dot_nt(a, b):
    return lax.dot_general(a, b, (((1,), (1,)), ((), ())), preferred_element_type=F32)


def _split2(a):
    hi = a.astype(BF16)
    lo = (a - hi.astype(F32)).astype(BF16)
    return hi, lo


def _dot3(a, b, nt=False):
    d = _dot_nt if nt else _dot
    ah, al = _split2(a)
    bh, bl = _split2(b)
    return d(ah, bh) + (d(al, bh) + d(ah, bl))


def _split3(a):
    p1 = a.astype(BF16)
    r1 = a - p1.astype(F32)
    p2 = r1.astype(BF16)
    r2 = r1 - p2.astype(F32)
    return p1, p2, r2.astype(BF16)


def _dot_sel(sel, x, nt=False):
    d = _dot_nt if nt else _dot
    p1, p2, p3 = _split3(x)
    return d(sel, p1) + (d(sel, p2) + d(sel, p3))


def _norm_proj_kernel(x_ref, g_ref, *refs, n_w):
    w_refs, o_refs = refs[:n_w], refs[n_w:]
    xn = _rms(x_ref[...], g_ref[...]).astype(BF16)
    for w_ref, o_ref in zip(w_refs, o_refs):
        n = w_ref.shape[1]
        for c in range(0, n, 512):
            cw = min(512, n - c)
            o_ref[:, c:c + cw] = _dot(xn, w_ref[:, c:c + cw])


def norm_proj(x, gain, ws, tm):
    r, d = x.shape
    tm = min(tm, r)
    assert r % tm == 0
    n_w = len(ws)
    return pl.pallas_call(
        functools.partial(_norm_proj_kernel, n_w=n_w),
        grid=(r // tm,),
        in_specs=[pl.BlockSpec((tm, d), lambda i: (i, 0)), _resident((1, d))]
        + [_resident(w.shape) for w in ws],
        out_specs=[pl.BlockSpec((tm, w.shape[1]), lambda i: (i, 0)) for w in ws],
        out_shape=[jax.ShapeDtypeStruct((r, w.shape[1]), F32) for w in ws],
        compiler_params=_cparams(("parallel",)),
        name="norm_proj",
    )(x, gain.reshape(1, d), *ws)


def _out_proj_kernel(a_ref, w_ref, h_ref, o_ref):
    o_ref[...] = h_ref[...] + _dot(a_ref[...].astype(BF16), w_ref[...])


def out_proj_res(a, w, h, tm):
    r, k = a.shape
    d = w.shape[1]
    tm = min(tm, r)
    assert r % tm == 0
    return pl.pallas_call(
        _out_proj_kernel,
        grid=(r // tm,),
        in_specs=[pl.BlockSpec((tm, k), lambda i: (i, 0)), _resident(w.shape),
                  pl.BlockSpec((tm, d), lambda i: (i, 0))],
        out_specs=pl.BlockSpec((tm, d), lambda i: (i, 0)),
        out_shape=jax.ShapeDtypeStruct((r, d), F32),
        compiler_params=_cparams(("parallel",)),
        name="out_proj_res",
    )(a, w, h)


def _shift_rows(u, head, stride):
    rows = u.shape[0]
    if stride % SUBLANES == 0:
        return jnp.concatenate([head, u[:rows - stride]], axis=0)
    assert stride == 1
    rolled = pltpu.roll(u, 1, axis=0)
    first = jnp.where(lax.broadcasted_iota(jnp.int32, (SUBLANES, u.shape[1]), 0) == 0, head, rolled[:SUBLANES])
    return jnp.concatenate([first, rolled[SUBLANES:]], axis=0)


def _ffn_kernel(x_ref, buf_ref, g_ref, wup_ref, cw_ref, wdn_ref, fg_ref, o_ref, tail_ref, carry, hid,
                *, tm, stride, pad, d_ff, final_norm, chunk):
    t = pl.program_id(1)

    @pl.when(t == 0)
    def _():
        carry[...] = buf_ref[0]

    x = x_ref[0]
    xn = _rms(x, g_ref[...]).astype(BF16)
    for j0 in range(0, d_ff, chunk):
        cs = []
        for c0 in (j0, d_ff + j0):
            cols = slice(c0, c0 + chunk)
            u = _dot(xn, wup_ref[:, cols])
            prev = carry[:, cols]
            u1 = _shift_rows(u, prev[pad - stride:], stride)
            u2 = _shift_rows(u1, prev[pad - 2 * stride:pad - stride], stride)
            cs.append(u2 * cw_ref[0:1, cols] + u1 * cw_ref[1:2, cols] + u * cw_ref[2:3, cols])
            carry[:, cols] = u[tm - pad:]
        hid[:, j0:j0 + chunk] = (_silu(cs[0]) * cs[1]).astype(BF16)
    out = x + _dot(hid[...], wdn_ref[...])
    if final_norm:
        out = _rms(out, fg_ref[...])
    o_ref[0] = out
    tail_ref[0] = carry[...]


def conv_ffn(h, buf, gain, w_up, conv_w, w_down, final_gain, *, tm, stride, final_norm):
    nb, t, d = h.shape
    pad = buf.shape[1]
    d_ff = w_down.shape[0]
    tm = min(tm, t)
    assert t % tm == 0 and pad % SUBLANES == 0 and pad >= 2 * stride and tm >= pad
    kern = functools.partial(_ffn_kernel, tm=tm, stride=stride, pad=pad, d_ff=d_ff,
                             final_norm=final_norm, chunk=256)
    return pl.pallas_call(
        kern,
        grid=(nb, t // tm),
        in_specs=[pl.BlockSpec((1, tm, d), lambda b, i: (b, i, 0)),
                  pl.BlockSpec((1, pad, 2 * d_ff), lambda b, i: (b, 0, 0)),
                  _resident((1, d)), _resident(w_up.shape), _resident(conv_w.shape),
                  _resident(w_down.shape), _resident((1, d))],
        out_specs=[pl.BlockSpec((1, tm, d), lambda b, i: (b, i, 0)),
                   pl.BlockSpec((1, pad, 2 * d_ff), lambda b, i: (b, 0, 0))],
        out_shape=[jax.ShapeDtypeStruct((nb, t, d), F32),
                   jax.ShapeDtypeStruct((nb, pad, 2 * d_ff), F32)],
        scratch_shapes=[pltpu.VMEM((pad, 2 * d_ff), F32), pltpu.VMEM((tm, d_ff), BF16)],
        compiler_params=_cparams(("parallel", "arbitrary")),
        name="conv_ffn",
    )(h, buf, gain.reshape(1, d), w_up, conv_w, w_down, final_gain.reshape(1, d))


def _xattn_kernel(x_ref, mk_ref, mv_ref, g_ref, wq_ref, wo_ref, o_ref, att, *, nb, tm):
    x = x_ref[...]
    xn = _rms(x, g_ref[...]).astype(BF16)
    q = _dot(xn, wq_ref[...])
    scale = XA_DH ** -0.5
    for b in range(nb):
        rows = slice(b * tm, (b + 1) * tm)
        for hh in range(XA_HEADS):
            cols = slice(hh * XA_DH, (hh + 1) * XA_DH)
            s = _dot_nt(q[rows, cols].astype(BF16), mk_ref[b, :, cols].astype(BF16)) * scale
            p = jnp.exp(s - jnp.max(s, axis=-1, keepdims=True))
            p = p / jnp.sum(p, axis=-1, keepdims=True)
            att[rows, cols] = _dot(p.astype(BF16), mv_ref[b, :, cols].astype(BF16))
    o_ref[...] = x + _dot(att[...].astype(BF16), wo_ref[...])


def mem_xattn(h, mem_k, mem_v, gain, w_q, w_o, *, nb, tm):
    r, d = h.shape
    n_seq, m, w = mem_k.shape
    t = r // n_seq
    assert t % tm == 0 and (nb == 1 or tm == t) and n_seq % nb == 0
    tiles = t // tm
    kern = functools.partial(_xattn_kernel, nb=nb, tm=tm)
    return pl.pallas_call(
        kern,
        grid=(n_seq // nb, tiles),
        in_specs=[pl.BlockSpec((nb * tm, d), lambda b, i: (b * tiles + i, 0)),
                  pl.BlockSpec((nb, m, w), lambda b, i: (b, 0, 0)),
                  pl.BlockSpec((nb, m, w), lambda b, i: (b, 0, 0)),
                  _resident((1, d)), _resident(w_q.shape), _resident(w_o.shape)],
        out_specs=pl.BlockSpec((nb * tm, d), lambda b, i: (b * tiles + i, 0)),
        out_shape=jax.ShapeDtypeStruct((r, d), F32),
        scratch_shapes=[pltpu.VMEM((nb * tm, w), F32)],
        compiler_params=_cparams(("parallel", "arbitrary")),
        name="mem_xattn",
    )(h, mem_k, mem_v, gain.reshape(1, d), w_q, w_o)


def _tri_inv(a, c):
    r = lax.broadcasted_iota(jnp.int32, (c, c), 0)
    q = lax.broadcasted_iota(jnp.int32, (c, c), 1)
    eye = (r == q).astype(F32)
    x = [eye - jnp.where((r >> 1) == (q >> 1), ai, 0.0) for ai in a]
    sh = 1
    while (1 << sh) < c:
        sh += 1
        mask = ((r >> sh) == (q >> sh)) & ((r >> (sh - 1)) != (q >> (sh - 1)))
        xb = [xi.astype(BF16) for xi in x]
        lx = [_dot(jnp.where(mask, ai, 0.0).astype(BF16), xbi) for ai, xbi in zip(a, xb)]
        x = [xi - _dot(xbi, li.astype(BF16)) for xi, xbi, li in zip(x, xb, lx)]
    return x


def _page_means_side_job(pt_ref, pool_hbm, km_ref, pbuf, psem, *, step, n_steps, layer, pps):
    slot = step % 2

    def copies(st, sl):
        return [pltpu.make_async_copy(pool_hbm.at[layer, pt_ref[st * pps + u]], pbuf.at[sl, u], psem.at[sl])
                for u in range(pps)]

    @pl.when(step == 0)
    def _():
        for cp in copies(step, slot):
            cp.start()

    @pl.when(step + 1 < n_steps)
    def _():
        for cp in copies(step + 1, 1 - slot):
            cp.start()

    for cp in copies(step, slot):
        cp.wait()
    for blk in range(pps // PPB):
        tot = jnp.sum(pbuf[slot, blk * PPB], axis=0)
        for w in range(1, PPB):
            tot = tot + jnp.sum(pbuf[slot, blk * PPB + w], axis=0)
        km_ref[0, blk] = tot * (1.0 / MOBA_BLOCK)


def _dn_kernel(*refs, tt, c, t_valid, nt, hps, pages):
    if pages is not None:
        pt_ref, refs = refs[0], refs[1:]
    (q_ref, k_ref, v_ref, z_ref, ba_ref, bq_ref, bk_ref, bv_ref, s0_ref,
     cwq_ref, cwk_ref, cwv_ref, avec_ref, dtb_ref, og_ref) = refs[:15]
    refs = refs[15:]
    if pages is not None:
        pool_hbm, o_ref, sfin_ref, km_ref, xq, xk, xv, state, pbuf, psem = refs
    else:
        o_ref, sfin_ref, xq, xk, xv, state = refs
    hg = pl.program_id(1)
    t = pl.program_id(2)
    lead = SUBLANES
    if pages is not None:
        step = (pl.program_id(0) * pl.num_programs(1) + hg) * nt + t
        _page_means_side_job(pt_ref, pool_hbm, km_ref, pbuf, psem, step=step, **pages)

    @pl.when(t == 0)
    def _():
        state[...] = s0_ref[0]
        xq[...] = bq_ref[0]
        xk[...] = bk_ref[0]
        xv[...] = bv_ref[0]

    def conv_silu(hist, x_ref, cw_ref):
        x = x_ref[0]
        taps = [x]
        for j in range(1, DN_CONV):
            taps.append(_shift_rows(taps[-1], hist[lead - j:lead - j + 1], 1))
        y = taps[DN_CONV - 1] * cw_ref[0:1]
        for j in range(1, DN_CONV):
            y = y + taps[DN_CONV - 1 - j] * cw_ref[j:j + 1]
        hist[...] = x[tt - lead:]
        return _silu(y)

    qa = conv_silu(xq, q_ref, cwq_ref)
    ka = conv_silu(xk, k_ref, cwk_ref)
    va = conv_silu(xv, v_ref, cwv_ref)

    ba = ba_ref[0]
    lane = lax.broadcasted_iota(jnp.int32, (tt, LANES), 1)
    beta_all = jax.nn.sigmoid(ba)
    g_all = -avec_ref[...] * jax.nn.softplus(ba + dtb_ref[...])
    if t_valid < nt * tt:
        row = t * tt + lax.broadcasted_iota(jnp.int32, (tt, LANES), 0)
        beta_all = jnp.where(row < t_valid, beta_all, 0.0)
        g_all = jnp.where(row < t_valid, g_all, 0.0)

    ri = lax.broadcasted_iota(jnp.int32, (c, c), 0)
    ci = lax.broadcasted_iota(jnp.int32, (c, c), 1)
    causal = ri >= ci
    strict = ri > ci
    ltri = causal.astype(BF16)
    ones8 = jnp.ones((SUBLANES, LANES), BF16)
    og = og_ref[...]

    q, k, v, beta, g_sel = [], [], [], [], []
    for hh in range(hps):
        cols = slice(hh * DN_DK, (hh + 1) * DN_DK)
        head = hg * hps + hh
        qh, kh = qa[:, cols], ka[:, cols]
        q.append(qh * lax.rsqrt(jnp.sum(qh * qh, axis=-1, keepdims=True) + EPS) * (DN_DK ** -0.5))
        k.append(kh * lax.rsqrt(jnp.sum(kh * kh, axis=-1, keepdims=True) + EPS))
        v.append(va[:, hh * DN_DV:(hh + 1) * DN_DV])
        beta.append(jnp.sum(jnp.where(lane == head, beta_all, 0.0), axis=1, keepdims=True))
        g_sel.append(jnp.where(lane == head + DN_HEADS, g_all, 0.0))

    n_ch = tt // c
    items = [(hh, slice(ch * c, (ch + 1) * c)) for ch in range(n_ch) for hh in range(hps)]
    gcs = [_dot_sel(ltri, g_sel[hh][sl]) for hh, sl in items]
    gcols = [jnp.sum(gc, axis=1, keepdims=True) for gc in gcs]
    grows = [_dot_sel(ones8, gc, nt=True)[0:1] for gc in gcs]
    egs = [jnp.exp(gcol) for gcol in gcols]
    kbs = [k[hh][sl] * beta[hh][sl] for hh, sl in items]
    kqs = [_dot_nt(jnp.concatenate([kb, q[hh][sl]], axis=0).astype(BF16), k[hh][sl].astype(BF16))
           for kb, (hh, sl) in zip(kbs, items)]
    gams = [jnp.exp(jnp.where(causal, gcol - grow, NEG)) for gcol, grow in zip(gcols, grows)]
    t_invs = _tri_inv([jnp.where(strict, kq[:c] * gam, 0.0) for kq, gam in zip(kqs, gams)], c)
    uws = [_dot(ti.astype(BF16),
                jnp.concatenate([v[hh][sl] * beta[hh][sl], kb * eg], axis=1).astype(BF16)).astype(BF16)
           for ti, (hh, sl), kb, eg in zip(t_invs, items, kbs, egs)]
    qkuws = [_dot((kq[c:] * gam).astype(BF16), uw) for kq, gam, uw in zip(kqs, gams, uws)]
    glasts = [gcol[c - 1:c] for gcol in gcols]
    kduws = [lax.dot_general((k[hh][sl] * jnp.exp(glast - gcol)).astype(BF16), uw, (((0,), (0,)), ((), ())),
                             preferred_element_type=F32)
             for (hh, sl), glast, gcol, uw in zip(items, glasts, gcols, uws)]
    lhss = [jnp.concatenate([q[hh][sl] * eg - qkuw[:, DN_DV:], kduw[:, DN_DV:]], axis=0).astype(BF16)
            for (hh, sl), eg, qkuw, kduw in zip(items, egs, qkuws, kduws)]

    s = [state[hh] for hh in range(hps)]
    for (hh, sl), lhs, qkuw, kduw, glast in zip(items, lhss, qkuws, kduws, glasts):
        r = _dot(lhs, s[hh].astype(BF16))
        o = r[:c] + qkuw[:, :DN_DV]
        s[hh] = s[hh] * jnp.exp(glast) - r[c:] + kduw[:, :DN_DV]
        o = o * lax.rsqrt(jnp.mean(o * o, axis=-1, keepdims=True) + EPS) * og
        cols = slice(hh * DN_DV, (hh + 1) * DN_DV)
        o_ref[0, sl, cols] = o * _silu(z_ref[0, sl, cols])
    for hh in range(hps):
        state[hh] = s[hh]

    @pl.when(t == nt - 1)
    def _():
        sfin_ref[0] = state[...]


def deltanet_core(qkv, z, ba, buf, s0, conv_w, a_log, dt_bias, o_gain, *, tt, c, t_valid, hps,
                  pool=None, pool_layer=0, page_table=None):
    b, tp, _ = qkv.shape
    hh = DN_HEADS
    nt = tp // tt
    assert tp % tt == 0 and tt % c == 0 and hh % hps == 0
    ng = hh // hps
    zpad = jnp.zeros((LANES - 2 * hh,), F32)
    avec = jnp.concatenate([jnp.zeros((hh,), F32), jnp.exp(a_log), zpad]).reshape(1, LANES)
    dtb = jnp.concatenate([jnp.zeros((hh,), F32), dt_bias, zpad]).reshape(1, LANES)
    wq, wv = hps * DN_DK, hps * DN_DV
    col = lambda part: pl.BlockSpec((1, tt, wq), lambda i, g, t, *_: (i, t, part * ng + g))
    bufc = lambda part: pl.BlockSpec((1, SUBLANES, wq), lambda i, g, t, *_: (i, 0, part * ng + g))
    cwc = lambda part: pl.BlockSpec((DN_CONV, wq), lambda i, g, t, *_: (0, part * ng + g))
    in_specs = [col(0), col(1), col(2),
                pl.BlockSpec((1, tt, wv), lambda i, g, t, *_: (i, t, g)),
                pl.BlockSpec((1, tt, LANES), lambda i, g, t, *_: (i, t, 0)),
                bufc(0), bufc(1), bufc(2),
                pl.BlockSpec((1, hps, DN_DK, DN_DV), lambda i, g, t, *_: (i, g, 0, 0)),
                cwc(0), cwc(1), cwc(2),
                pl.BlockSpec((1, LANES), lambda i, g, t, *_: (0, 0)),
                pl.BlockSpec((1, LANES), lambda i, g, t, *_: (0, 0)),
                pl.BlockSpec((1, DN_DV), lambda i, g, t, *_: (0, 0))]
    out_specs = [pl.BlockSpec((1, tt, wv), lambda i, g, t, *_: (i, t, g)),
                 pl.BlockSpec((1, hps, DN_DK, DN_DV), lambda i, g, t, *_: (i, g, 0, 0))]
    out_shape = [jax.ShapeDtypeStruct((b, tp, hh * DN_DV), F32),
                 jax.ShapeDtypeStruct((b, hh, DN_DK, DN_DV), F32)]
    scratch = [pltpu.VMEM((SUBLANES, wq), F32), pltpu.VMEM((SUBLANES, wq), F32),
               pltpu.VMEM((SUBLANES, wv), F32), pltpu.VMEM((hps, DN_DK, DN_DV), F32)]
    args = [qkv, qkv, qkv, z, ba, buf, buf, buf, s0, conv_w, conv_w, conv_w, avec, dtb,
            o_gain.reshape(1, DN_DV)]
    pages = None
    prefetch = []
    sem = ("parallel", "parallel", "arbitrary")
    if pool is not None:
        n_steps = b * ng * nt
        n_seq, n_pages = page_table.shape
        _, _, page, ph, pdh = pool.shape
        assert (n_seq * n_pages) % n_steps == 0
        pps = (n_seq * n_pages) // n_steps
        assert pps % PPB == 0 and n_pages % pps == 0
        pages = dict(n_steps=n_steps, layer=pool_layer, pps=pps)
        prefetch = [page_table.reshape(-1)]
        in_specs.append(pl.BlockSpec(memory_space=pl.ANY))
        args.append(pool)
        out_specs.append(pl.BlockSpec((1, pps // PPB, ph, pdh),
                                      lambda i, g, t, *_: ((i * ng + g) * nt + t, 0, 0, 0)))
        out_shape.append(jax.ShapeDtypeStruct((n_steps, pps // PPB, ph, pdh), F32))
        scratch += [pltpu.VMEM((2, pps, page, ph, pdh), F32), pltpu.SemaphoreType.DMA((2,))]
        sem = ("arbitrary", "arbitrary", "arbitrary")
    kern = functools.partial(_dn_kernel, tt=tt, c=c, t_valid=t_valid, nt=nt, hps=hps, pages=pages)
    res = pl.pallas_call(
        kern,
        grid_spec=pltpu.PrefetchScalarGridSpec(
            num_scalar_prefetch=len(prefetch), grid=(b, ng, nt),
            in_specs=in_specs, out_specs=out_specs, scratch_shapes=scratch),
        out_shape=out_shape,
        compiler_params=_cparams(sem),
        name="deltanet_core",
    )(*prefetch, *args)
    if pool is None:
        return res[0], res[1], None
    return res[0], res[1], res[2].reshape(n_seq, n_pages // PPB, ph, pdh)


def _topk_mask_rows(g, n_valid, n_rows, k_top):
    jrow = lax.broadcasted_iota(jnp.int32, g.shape, 0)
    cnt = jnp.zeros(g.shape, F32)
    for jp in range(min(n_rows, n_valid)):
        gb = g[jp:jp + 1, :]
        cnt = cnt + jnp.where(gb > g, 1.0, jnp.where(gb == g, jnp.where(jrow > jp, 1.0, 0.0), 0.0))
    return jnp.where(jrow < n_valid, cnt, float(k_top)) < float(k_top), cnt


LOG2E = 1.4426950408889634
N_EXT = 16


def _split3_f32(a):
    p1 = a.astype(BF16).astype(F32)
    r1 = a - p1
    p2 = r1.astype(BF16).astype(F32)
    return p1, p2, (r1 - p2).astype(BF16).astype(F32)


def _moba_kernel(q_ref, k_ref, v_ref, sl_ref, o_ref, kmean, kaug, vt, qtb_s, pen, sbuf, *, nb, nbp, grp):
    i = pl.program_id(2)
    blk, dh = MOBA_BLOCK, MOBA_DH
    slope = sl_ref[0]

    @pl.when(i == 0)
    def _():
        kmean[...] = jnp.zeros_like(kmean)
        lane = lax.broadcasted_iota(jnp.int32, (blk, LANES), 1)
        crow = lax.broadcasted_iota(jnp.int32, (blk, LANES), 0).astype(F32)
        kext = jnp.where(lane < 3, crow, jnp.where(lane < 6, 1.0, 0.0)).astype(BF16)
        for j in range(nb):
            rows = slice(j * blk, (j + 1) * blk)
            kj = k_ref[0, rows, :]
            kmean[j:j + 1, :] = jnp.sum(kj, axis=0, keepdims=True) * (1.0 / blk)
            kaug[j] = jnp.concatenate([kj.astype(BF16), kext], axis=1)
            vt[j] = v_ref[0, rows, :].T.astype(BF16)
        km = kmean[...]
        jrow = lax.broadcasted_iota(jnp.int32, (nbp, blk), 0)
        ridx = lax.broadcasted_iota(jnp.int32, (nbp, blk), 1)
        for jq in range(nb):
            qt = (q_ref[0, jq * blk:(jq + 1) * blk, :] * (dh ** -0.5)).T
            qtb_s[jq] = (qt * LOG2E).astype(BF16)
            sel, _ = _topk_mask_rows(_dot3(km, qt), jq, nb, MOBA_TOPK)
            add = jnp.where(jrow == jq, 0.0, jnp.where(sel, 0.0, NEG))
            row_j = LOG2E * (add - slope * ((jq - jrow) * blk + ridx).astype(F32))
            for n, piece in enumerate(_split3_f32(row_j)):
                pen[n, jq] = piece

    qtb = qtb_s[i]
    slope_pieces = _split3_f32(slope * LOG2E)
    row16 = lax.broadcasted_iota(jnp.int32, (N_EXT, blk), 0)
    zpad = jnp.zeros((LANES - N_EXT, blk), BF16)

    def q_aug(pieces):
        ext = jnp.zeros((N_EXT, blk), F32)
        for n, piece in enumerate(tuple(slope_pieces) + tuple(pieces)):
            ext = jnp.where(row16 == n, piece, ext)
        return jnp.concatenate([qtb, ext.astype(BF16), zpad], axis=0)

    key_minus_qry = (lax.broadcasted_iota(jnp.int32, (blk, blk), 0)
                     - lax.broadcasted_iota(jnp.int32, (blk, blk), 1))

    def fold8(x, op):
        return op(x.reshape(blk // SUBLANES, SUBLANES, blk), axis=0)

    n_trips = i // grp + 1

    def pass_scores(g, m8):
        for u in range(grp):
            j = g * grp + u
            s = _dot(kaug[j], q_aug([pen[n, i, pl.ds(j, 1), :] for n in range(3)]))
            s = jnp.where(key_minus_qry > jnp.where(j == i, 0, blk), NEG, s)
            sbuf[j] = s
            m8 = jnp.maximum(m8, fold8(s, jnp.max))
        return m8

    m8 = lax.fori_loop(0, n_trips, pass_scores, jnp.full((SUBLANES, blk), NEG, F32))
    m = jnp.max(m8, axis=0, keepdims=True)

    def pass_values(g, carry):
        l8, acc = carry
        for u in range(grp):
            j = g * grp + u
            p = jnp.exp2(sbuf[j] - m)
            l8 = l8 + fold8(p, jnp.sum)
            acc = acc + _dot(vt[j], p.astype(BF16))
        return l8, acc

    l8, acc = lax.fori_loop(0, n_trips, pass_values,
                            (jnp.zeros((SUBLANES, blk), F32), jnp.zeros((dh, blk), F32)))
    o_ref[0] = (acc / jnp.sum(l8, axis=0, keepdims=True)).T


def _alibi_rows(n_heads, width):
    slopes = jnp.exp2(-8.0 * jnp.arange(1, n_heads + 1, dtype=F32) / n_heads)
    return jnp.broadcast_to(slopes[:, None, None], (n_heads, 1, width))


def moba_prompt_attn(q, k, v):
    b, t, w = q.shape
    hh, dh, blk = MOBA_HEADS, MOBA_DH, MOBA_BLOCK
    assert t % blk == 0
    nb = t // blk
    nbp = -(-nb // SUBLANES) * SUBLANES
    grp = next(g for g in (4, 2, 1) if nb % g == 0)
    kern = functools.partial(_moba_kernel, nb=nb, nbp=nbp, grp=grp)
    return pl.pallas_call(
        kern,
        grid=(b, hh, nb),
        in_specs=[pl.BlockSpec((1, t, dh), lambda bi, h, i: (bi, 0, h)),
                  pl.BlockSpec((1, t, dh), lambda bi, h, i: (bi, 0, h)),
                  pl.BlockSpec((1, t, dh), lambda bi, h, i: (bi, 0, h)),
                  pl.BlockSpec((1, 1, blk), lambda bi, h, i: (h, 0, 0))],
        out_specs=pl.BlockSpec((1, blk, dh), lambda bi, h, i: (bi, i, h)),
        out_shape=jax.ShapeDtypeStruct((b, t, w), F32),
        scratch_shapes=[pltpu.VMEM((nbp, dh), F32), pltpu.VMEM((nb, blk, dh + LANES), BF16),
                        pltpu.VMEM((nb, dh, blk), BF16), pltpu.VMEM((nb, dh, blk), BF16),
                        pltpu.VMEM((3, nb, nbp, blk), F32), pltpu.VMEM((nb, blk, blk), F32)],
        compiler_params=_cparams(("parallel", "parallel", "arbitrary")),
        name="moba_prompt_attn",
    )(q, k, v, _alibi_rows(hh, blk))


PAGES_PER_STEP = 16
PPB = MOBA_BLOCK // PAGE_SIZE


def _kmean_kernel(pt_ref, *refs):
    page_refs, o_ref = refs[:PAGES_PER_STEP], refs[PAGES_PER_STEP]
    for u in range(0, PAGES_PER_STEP, PPB):
        tot = jnp.sum(page_refs[u][...], axis=0)
        for w in range(1, PPB):
            tot = tot + jnp.sum(page_refs[u + w][...], axis=0)
        o_ref[0, u // PPB] = tot * (1.0 / MOBA_BLOCK)


def moba_block_means(pool_k, layer, page_table):
    n_seq, n_pages = page_table.shape
    _, _, page, hh, dh = pool_k.shape
    assert n_pages % PAGES_PER_STEP == 0 and PAGES_PER_STEP % PPB == 0
    steps = n_pages // PAGES_PER_STEP

    def page_spec(u):
        return pl.BlockSpec((None, None, page, hh, dh),
                            lambda b, g, pt: (layer, pt[b * n_pages + g * PAGES_PER_STEP + u], 0, 0, 0))

    bps = PAGES_PER_STEP // PPB
    return pl.pallas_call(
        _kmean_kernel,
        grid_spec=pltpu.PrefetchScalarGridSpec(
            num_scalar_prefetch=1, grid=(n_seq, steps),
            in_specs=[page_spec(u) for u in range(PAGES_PER_STEP)],
            out_specs=pl.BlockSpec((1, bps, hh, dh), lambda b, g, pt: (b, g, 0, 0))),
        out_shape=jax.ShapeDtypeStruct((n_seq, n_pages // PPB, hh, dh), F32),
        compiler_params=_cparams(("parallel", "arbitrary")),
        name="moba_block_means",
    )(page_table.reshape(-1), *([pool_k] * PAGES_PER_STEP))


def _moba_pick_kernel(q_ref, km_ref, o_ref, *, n_blk, t):
    hh, dh = MOBA_HEADS, MOBA_DH
    q = q_ref[0] * (dh ** -0.5)
    cols = []
    for h in range(hh):
        sl = slice(h * dh, (h + 1) * dh)
        cols.append(_dot3(km_ref[0, :, sl], q[:, sl], nt=True))
    gate = jnp.concatenate(cols, axis=1)
    _, cnt = _topk_mask_rows(gate, n_blk, n_blk, MOBA_TOPK)
    jrow = lax.broadcasted_iota(jnp.int32, gate.shape, 0).astype(F32)
    rows = [jnp.sum(jnp.where(cnt == float(r), jrow, 0.0), axis=0, keepdims=True)
            for r in range(MOBA_TOPK)]
    rows.append(jnp.zeros((SUBLANES - MOBA_TOPK, hh * t), F32))
    o_ref[0] = jnp.concatenate(rows, axis=0).astype(jnp.int32)


def moba_pick_blocks(q, kmean):
    n_seq, t, w = q.shape
    n_blk = kmean.shape[1]
    kern = functools.partial(_moba_pick_kernel, n_blk=n_blk, t=t)
    return pl.pallas_call(
        kern,
        grid=(n_seq,),
        in_specs=[pl.BlockSpec((1, t, w), lambda b: (b, 0, 0)),
                  pl.BlockSpec((1, n_blk, w), lambda b: (b, 0, 0))],
        out_specs=pl.BlockSpec((1, SUBLANES, MOBA_HEADS * t), lambda b: (b, 0, 0)),
        out_shape=jax.ShapeDtypeStruct((n_seq, SUBLANES, MOBA_HEADS * t), jnp.int32),
        compiler_params=_cparams(("parallel",)),
        name="moba_pick_blocks",
    )(q, kmean)


def _moba_dec_kernel(pt_ref, sel_ref, q_ref, kn_ref, vn_ref, sl_ref, pk_hbm, pv_hbm, o_ref,
                     kbuf, vbuf, sem, *, layer, t_len, past, n_pages, n_heads, n_steps):
    n_pg = MOBA_TOPK * PPB
    b, h = pl.program_id(0), pl.program_id(1)
    step = b * n_heads + h
    slot = step % 2
    page = PAGE_SIZE

    def blk_of(bb, hh, t, r):
        return sel_ref[((bb * MOBA_TOPK + r) * n_heads + hh) * t_len + t]

    def copies(bb, hh, sl):
        out = []
        for t in range(t_len):
            for r in range(MOBA_TOPK):
                for u in range(PPB):
                    phys = pt_ref[bb * n_pages + blk_of(bb, hh, t, r) * PPB + u]
                    i = (t * MOBA_TOPK + r) * PPB + u
                    out.append(pltpu.make_async_copy(pk_hbm.at[layer, phys, :, hh, :], kbuf.at[sl, i], sem.at[0, sl]))
                    out.append(pltpu.make_async_copy(pv_hbm.at[layer, phys, :, hh, :], vbuf.at[sl, i], sem.at[1, sl]))
        return out

    @pl.when(step == 0)
    def _():
        for cp in copies(b, h, slot):
            cp.start()

    @pl.when(step + 1 < n_steps)
    def _():
        nxt = step + 1
        for cp in copies(nxt // n_heads, nxt % n_heads, 1 - slot):
            cp.start()

    slope = sl_ref[0][:, :page]
    q = (q_ref[0] * (MOBA_DH ** -0.5)).astype(BF16)
    row = lax.broadcasted_iota(jnp.int32, (t_len, page), 0)
    lane = lax.broadcasted_iota(jnp.int32, (t_len, page), 1)
    s_own = _dot_nt(q, kn_ref[0].astype(BF16))
    qi = lax.broadcasted_iota(jnp.int32, (t_len, t_len), 0)
    ki = lax.broadcasted_iota(jnp.int32, (t_len, t_len), 1)
    s_own = jnp.where(ki <= qi, s_own - slope[:, :t_len] * (qi - ki).astype(F32), NEG)

    for cp in copies(b, h, slot):
        cp.wait()

    scores = [s_own]
    for t in range(t_len):
        for r in range(MOBA_TOPK):
            for u in range(PPB):
                dist = (past + row) - (blk_of(b, h, t, r) * MOBA_BLOCK + u * page + lane)
                s = _dot_nt(q, kbuf[slot, (t * MOBA_TOPK + r) * PPB + u].astype(BF16))
                scores.append(jnp.where(row == t, s - slope * dist.astype(F32), NEG))
    def tree(xs, op):
        while len(xs) > 1:
            xs = [op(xs[n], xs[n + 1]) for n in range(0, len(xs) - 1, 2)] + ([xs[-1]] if len(xs) % 2 else [])
        return xs[0]

    m = jnp.maximum(scores[0].max(axis=-1, keepdims=True),
                    tree(scores[1:], jnp.maximum).max(axis=-1, keepdims=True))
    ps = [jnp.exp(s - m) for s in scores]
    l = ps[0].sum(axis=-1, keepdims=True) + tree(ps[1:], jnp.add).sum(axis=-1, keepdims=True)
    inv_l = 1.0 / l
    acc = _dot((ps[0] * inv_l).astype(BF16), vn_ref[0].astype(BF16))
    for i in range(t_len * n_pg):
        acc = acc + _dot((ps[1 + i] * inv_l).astype(BF16), vbuf[slot, i].astype(BF16))
    o_ref[0, 0] = acc


def moba_decode_attn(q, k_new, v_new, pool_k, pool_v, layer, page_table, sel):
    n_seq, t_len, w = q.shape
    n_pages = page_table.shape[1]
    hh, dh = MOBA_HEADS, MOBA_DH
    _, _, page, _, _ = pool_k.shape
    past = n_pages * page
    assert past % MOBA_BLOCK == 0
    n_buf = t_len * MOBA_TOPK * PPB
    row = lambda: pl.BlockSpec((1, t_len, dh), lambda b, h, pt, sl: (b, 0, h))
    kern = functools.partial(_moba_dec_kernel, layer=layer, t_len=t_len, past=past, n_pages=n_pages,
                             n_heads=hh, n_steps=n_seq * hh)
    return pl.pallas_call(
        kern,
        grid_spec=pltpu.PrefetchScalarGridSpec(
            num_scalar_prefetch=2, grid=(n_seq, hh),
            in_specs=[row(), row(), row(),
                      pl.BlockSpec((1, 1, MOBA_BLOCK), lambda b, h, pt, sl: (h, 0, 0)),
                      pl.BlockSpec(memory_space=pl.ANY), pl.BlockSpec(memory_space=pl.ANY)],
            out_specs=pl.BlockSpec((1, 1, t_len, dh), lambda b, h, pt, sl: (b, h, 0, 0)),
            scratch_shapes=[pltpu.VMEM((2, n_buf, page, dh), F32), pltpu.VMEM((2, n_buf, page, dh), F32),
                            pltpu.SemaphoreType.DMA((2, 2))]),
        out_shape=jax.ShapeDtypeStruct((n_seq, hh, t_len, dh), F32),
        compiler_params=_cparams(("arbitrary", "arbitrary")),
        name="moba_decode_attn",
    )(page_table.reshape(-1), sel.reshape(-1), q, k_new, v_new, _alibi_rows(hh, MOBA_BLOCK), pool_k, pool_v)


def _last_rows(buf, x, n):
    t = x.shape[1]
    if t >= n:
        return x[:, t - n:]
    return jnp.concatenate([buf[:, buf.shape[1] - (n - t):], x], axis=1)


def _pad_front(buf, rows):
    return jnp.pad(buf, ((0, 0), (rows - buf.shape[1], 0), (0, 0)))


def kernel(x_prompt, x_sample, state_dn_s, state_dn_conv, cache_moba_k, cache_moba_v, page_table, cache_mem_k, cache_mem_v, state_ffn_conv, mem_prompt, norm_mix, norm_xattn, norm_mem, norm_ffn, norm_final, dn_w_in, dn_conv_w, dn_a_log, dn_dt_bias, dn_o_gain, dn_w_out, moba_w_qkv, moba_w_out, xa_w_q, xa_w_kv, xa_w_out, ffn_w_up, ffn_conv_w, ffn_w_down):
    bp, seq, d = x_prompt.shape
    bs, dseq, _ = x_sample.shape
    depth = norm_mix.shape[0]
    mem_len = mem_prompt.shape[1]
    d_ff = ffn_w_down.shape[1]
    qk_w = DN_HEADS * DN_DK
    v_w = DN_HEADS * DN_DV
    conv_ch = 2 * qk_w + v_w
    moba_w = MOBA_HEADS * MOBA_DH
    xa_w = XA_HEADS * XA_DH
    bf = lambda a: a.astype(BF16)

    hp = x_prompt.reshape(bp * seq, d)
    hs = x_sample.reshape(bs * dseq, d)
    outs = {k: [] for k in ("p_dn_s", "p_dn_c", "p_mk", "p_mv", "p_memk", "p_memv", "p_ffc",
                            "s_dn_s", "s_dn_c", "s_mk", "s_mv", "s_ffc")}
    kmeans = {}
    for layer in range(depth):
        if layer % 2 == 0:
            ia = layer // 2
            w_in = dn_w_in[ia]
            ws = [bf(w_in[:, :conv_ch]), bf(w_in[:, conv_ch:conv_ch + v_w]),
                  bf(jnp.pad(w_in[:, conv_ch + v_w:], ((0, 0), (0, LANES - 2 * DN_HEADS))))]
            w_out = bf(dn_w_out[ia])
            args = (dn_conv_w[ia], dn_a_log[ia], dn_dt_bias[ia], dn_o_gain[ia])
            qkv, z, ba = norm_proj(hp, norm_mix[layer], ws, tm=512)
            qkv3 = qkv.reshape(bp, seq, conv_ch)
            tt, hps = 512, 2
            n_steps = bp * (DN_HEADS // hps) * (seq // tt)
            n_pg_all = page_table.shape[0] * page_table.shape[1]
            ride = (layer + 1 < depth and n_pg_all % n_steps == 0 and (n_pg_all // n_steps) % PPB == 0
                    and page_table.shape[1] % (n_pg_all // n_steps) == 0)
            o, s_fin, km = deltanet_core(qkv3, z.reshape(bp, seq, v_w), ba.reshape(bp, seq, LANES),
                                         jnp.zeros((bp, SUBLANES, conv_ch), F32),
                                         jnp.zeros((bp, DN_HEADS, DN_DK, DN_DV), F32), *args,
                                         tt=tt, c=DN_CHUNK, t_valid=seq, hps=hps,
                                         **(dict(pool=cache_moba_k, pool_layer=ia, page_table=page_table)
                                            if ride else {}))
            kmeans[ia] = km
            hp = out_proj_res(o.reshape(bp * seq, v_w), w_out, hp, tm=512)
            outs["p_dn_s"].append(s_fin)
            outs["p_dn_c"].append(_last_rows(jnp.zeros((bp, DN_CONV - 1, conv_ch), F32), qkv3, DN_CONV - 1))
            qkv, z, ba = norm_proj(hs, norm_mix[layer], ws, tm=256)
            qkv3 = qkv.reshape(bs, dseq, conv_ch)
            tpad = 16
            padt = lambda a: jnp.pad(a.reshape(bs, dseq, -1), ((0, 0), (0, tpad - dseq), (0, 0)))
            o, s_fin, _ = deltanet_core(padt(qkv), padt(z), padt(ba),
                                     _pad_front(state_dn_conv[ia], SUBLANES), state_dn_s[ia], *args,
                                     tt=tpad, c=tpad, t_valid=dseq, hps=DN_HEADS)
            hs = out_proj_res(o[:, :dseq].reshape(bs * dseq, v_w), w_out, hs, tm=256)
            outs["s_dn_s"].append(s_fin)
            outs["s_dn_c"].append(_last_rows(state_dn_conv[ia], qkv3, DN_CONV - 1))
        else:
            ib = layer // 2
            w_qkv = moba_w_qkv[ib]
            ws = [bf(w_qkv[:, j * moba_w:(j + 1) * moba_w]) for j in range(3)]
            w_out = bf(moba_w_out[ib])
            q, k, v = (a.reshape(bp, seq, moba_w) for a in norm_proj(hp, norm_mix[layer], ws, tm=512))
            o = moba_prompt_attn(q, k, v)
            hp = out_proj_res(o.reshape(bp * seq, moba_w), w_out, hp, tm=512)
            outs["p_mk"].append(k.reshape(bp, seq, MOBA_HEADS, MOBA_DH))
            outs["p_mv"].append(v.reshape(bp, seq, MOBA_HEADS, MOBA_DH))
            q, k, v = (a.reshape(bs, dseq, moba_w) for a in norm_proj(hs, norm_mix[layer], ws, tm=256))
            kmean = kmeans.get(ib)
            if kmean is None:
                kmean = moba_block_means(cache_moba_k, ib, page_table)
            pick = moba_pick_blocks(q, kmean.reshape(bs, -1, moba_w))
            sel = pick[:, :MOBA_TOPK].reshape(bs, MOBA_TOPK, MOBA_HEADS, dseq)
            o = moba_decode_attn(q, k, v, cache_moba_k, cache_moba_v, ib, page_table, sel)
            o = o.transpose(0, 2, 1, 3).reshape(bs * dseq, moba_w)
            hs = out_proj_res(o, w_out, hs, tm=256)
            outs["s_mk"].append(k.reshape(bs, dseq, MOBA_HEADS, MOBA_DH))
            outs["s_mv"].append(v.reshape(bs, dseq, MOBA_HEADS, MOBA_DH))

        w_kv = xa_w_kv[layer]
        mk, mv = norm_proj(mem_prompt.reshape(bp * mem_len, d), norm_mem[layer],
                           [bf(w_kv[:, :xa_w]), bf(w_kv[:, xa_w:])], tm=256)
        mk = mk.reshape(bp, mem_len, xa_w)
        mv = mv.reshape(bp, mem_len, xa_w)
        outs["p_memk"].append(mk.reshape(bp, mem_len, XA_HEADS, XA_DH))
        outs["p_memv"].append(mv.reshape(bp, mem_len, XA_HEADS, XA_DH))
        w_q, w_o = bf(xa_w_q[layer]), bf(xa_w_out[layer])
        hp = mem_xattn(hp, mk, mv, norm_xattn[layer], w_q, w_o, nb=1, tm=512)
        hs = mem_xattn(hs, cache_mem_k[layer].reshape(bs, mem_len, xa_w),
                       cache_mem_v[layer].reshape(bs, mem_len, xa_w),
                       norm_xattn[layer], w_q, w_o, nb=8, tm=dseq)

        last = layer == depth - 1
        w_up, w_dn = bf(ffn_w_up[layer]), bf(ffn_w_down[layer])
        hp3, tail = conv_ffn(hp.reshape(bp, seq, d), jnp.zeros((bp, SUBLANES, 2 * d_ff), F32),
                             norm_ffn[layer], w_up, ffn_conv_w[layer], w_dn, norm_final,
                             tm=512, stride=1, final_norm=last)
        hp = hp3.reshape(bp * seq, d)
        outs["p_ffc"].append(tail[:, SUBLANES - (FFN_CONV - 1):])
        hs_tm = hs.reshape(bs, dseq, d).transpose(1, 0, 2).reshape(1, dseq * bs, d)
        buf_tm = state_ffn_conv[layer].transpose(1, 0, 2).reshape(1, (FFN_CONV - 1) * bs, 2 * d_ff)
        hs3, tail = conv_ffn(hs_tm, buf_tm, norm_ffn[layer], w_up, ffn_conv_w[layer], w_dn, norm_final,
                             tm=dseq * bs, stride=bs, final_norm=last)
        hs = hs3.reshape(dseq, bs, d).transpose(1, 0, 2).reshape(bs * dseq, d)
        outs["s_ffc"].append(tail.reshape(FFN_CONV - 1, bs, 2 * d_ff).transpose(1, 0, 2))

    y_prompt = hp.reshape(bp, seq, d)
    y_sample = hs.reshape(bs, dseq, d)
    st = lambda k: outs[k][0][None] if len(outs[k]) == 1 else jnp.stack(outs[k])
    return (y_prompt, y_sample, st("p_dn_s"), st("p_dn_c"), st("p_mk"), st("p_mv"),
            st("p_memk"), st("p_memv"), st("p_ffc"),
            st("s_dn_s"), st("s_dn_c"), st("s_mk"), st("s_mv"), st("s_ffc"))
```

```python
import functools

import jax
import jax.numpy as jnp
from jax import lax
from jax.experimental import pallas as pl
from jax.experimental.pallas import tpu as pltpu

F32 = jnp.float32
BF16 = jnp.bfloat16
EPS = 1e-6
NEG = -1e30

DN_HEADS = 8
DN_DK = 128
DN_DV = 128
DN_CONV = 4
DN_CHUNK = 64
MOBA_HEADS = 8
MOBA_DH = 128
MOBA_BLOCK = 256
MOBA_TOPK = 3
PAGE_SIZE = 128
XA_HEADS = 4
XA_DH = 128
FFN_CONV = 3

SUBLANES = 8
LANES = 128
VMEM_LIMIT = 56 * 1024 * 1024


def _cparams(sem):
    return pltpu.CompilerParams(dimension_semantics=sem, vmem_limit_bytes=VMEM_LIMIT)


def _resident(shape):
    nd = len(shape)
    return pl.BlockSpec(shape, lambda *_: (0,) * nd, pipeline_mode=pl.Buffered(1))


def _rms(x, gain):
    return x * lax.rsqrt(jnp.mean(x * x, axis=-1, keepdims=True) + EPS) * gain


def _silu(x):
    return x * jax.nn.sigmoid(x)


def _dot(a, b):
    return jnp.dot(a, b, preferred_element_type=F32)


def _dot_nt(a, b):
    return lax.dot_general(a, b, (((1,), (1,)), ((), ())), preferred_element_type=F32)


def _split2(a):
    hi = a.astype(BF16)
    lo = (a - hi.astype(F32)).astype(BF16)
    return hi, lo


def _dot3(a, b, nt=False):
    d = _dot_nt if nt else _dot
    ah, al = _split2(a)
    bh, bl = _split2(b)
    return d(ah, bh) + (d(al, bh) + d(ah, bl))


def _split3(a):
    p1 = a.astype(BF16)
    r1 = a - p1.astype(F32)
    p2 = r1.astype(BF16)
    r2 = r1 - p2.astype(F32)
    return p1, p2, r2.astype(BF16)


def _dot_sel(sel, x, nt=False):
    d = _dot_nt if nt else _dot
    p1, p2, p3 = _split3(x)
    return d(sel, p1) + (d(sel, p2) + d(sel, p3))


def _norm_proj_kernel(x_ref, g_ref, *refs, n_w):
    w_refs, o_refs = refs[:n_w], refs[n_w:]
    xn = _rms(x_ref[...], g_ref[...]).astype(BF16)
    for w_ref, o_ref in zip(w_refs, o_refs):
        n = w_ref.shape[1]
        for c in range(0, n, 512):
            cw = min(512, n - c)
            o_ref[:, c:c + cw] = _dot(xn, w_ref[:, c:c + cw])


def norm_proj(x, gain, ws, tm):
    r, d = x.shape
    tm = min(tm, r)
    assert r % tm == 0
    n_w = len(ws)
    return pl.pallas_call(
        functools.partial(_norm_proj_kernel, n_w=n_w),
        grid=(r // tm,),
        in_specs=[pl.BlockSpec((tm, d), lambda i: (i, 0)), _resident((1, d))]
        + [_resident(w.shape) for w in ws],
        out_specs=[pl.BlockSpec((tm, w.shape[1]), lambda i: (i, 0)) for w in ws],
        out_shape=[jax.ShapeDtypeStruct((r, w.shape[1]), F32) for w in ws],
        compiler_params=_cparams(("parallel",)),
        name="norm_proj",
    )(x, gain.reshape(1, d), *ws)


def _out_proj_kernel(a_ref, w_ref, h_ref, o_ref):
    o_ref[...] = h_ref[...] + _dot(a_ref[...].astype(BF16), w_ref[...])


def out_proj_res(a, w, h, tm):
    r, k = a.shape
    d = w.shape[1]
    tm = min(tm, r)
    assert r % tm == 0
    return pl.pallas_call(
        _out_proj_kernel,
        grid=(r // tm,),
        in_specs=[pl.BlockSpec((tm, k), lambda i: (i, 0)), _resident(w.shape),
                  pl.BlockSpec((tm, d), lambda i: (i, 0))],
        out_specs=pl.BlockSpec((tm, d), lambda i: (i, 0)),
        out_shape=jax.ShapeDtypeStruct((r, d), F32),
        compiler_params=_cparams(("parallel",)),
        name="out_proj_res",
    )(a, w, h)


def _shift_rows(u, head, stride):
    rows = u.shape[0]
    if stride % SUBLANES == 0:
        return jnp.concatenate([head, u[:rows - stride]], axis=0)
    assert stride == 1
    rolled = pltpu.roll(u, 1, axis=0)
    first = jnp.where(lax.broadcasted_iota(jnp.int32, (SUBLANES, u.shape[1]), 0) == 0, head, rolled[:SUBLANES])
    return jnp.concatenate([first, rolled[SUBLANES:]], axis=0)


def _ffn_kernel(x_ref, buf_ref, g_ref, wup_ref, cw_ref, wdn_ref, fg_ref, o_ref, tail_ref, carry, hid,
                *, tm, stride, pad, d_ff, final_norm, chunk):
    t = pl.program_id(1)

    @pl.when(t == 0)
    def _():
        carry[...] = buf_ref[0]

    x = x_ref[0]
    xn = _rms(x, g_ref[...]).astype(BF16)
    for j0 in range(0, d_ff, chunk):
        cs = []
        for c0 in (j0, d_ff + j0):
            cols = slice(c0, c0 + chunk)
            u = _dot(xn, wup_ref[:, cols])
            prev = carry[:, cols]
            u1 = _shift_rows(u, prev[pad - stride:], stride)
            u2 = _shift_rows(u1, prev[pad - 2 * stride:pad - stride], stride)
            cs.append(u2 * cw_ref[0:1, cols] + u1 * cw_ref[1:2, cols] + u * cw_ref[2:3, cols])
            carry[:, cols] = u[tm - pad:]
        hid[:, j0:j0 + chunk] = (_silu(cs[0]) * cs[1]).astype(BF16)
    out = x + _dot(hid[...], wdn_ref[...])
    if final_norm:
        out = _rms(out, fg_ref[...])
    o_ref[0] = out
    tail_ref[0] = carry[...]


def conv_ffn(h, buf, gain, w_up, conv_w, w_down, final_gain, *, tm, stride, final_norm):
    nb, t, d = h.shape
    pad = buf.shape[1]
    d_ff = w_down.shape[0]
    tm = min(tm, t)
    assert t % tm == 0 and pad % SUBLANES == 0 and pad >= 2 * stride and tm >= pad
    kern = functools.partial(_ffn_kernel, tm=tm, stride=stride, pad=pad, d_ff=d_ff,
                             final_norm=final_norm, chunk=256)
    return pl.pallas_call(
        kern,
        grid=(nb, t // tm),
        in_specs=[pl.BlockSpec((1, tm, d), lambda b, i: (b, i, 0)),
                  pl.BlockSpec((1, pad, 2 * d_ff), lambda b, i: (b, 0, 0)),
                  _resident((1, d)), _resident(w_up.shape), _resident(conv_w.shape),
                  _resident(w_down.shape), _resident((1, d))],
        out_specs=[pl.BlockSpec((1, tm, d), lambda b, i: (b, i, 0)),
                   pl.BlockSpec((1, pad, 2 * d_ff), lambda b, i: (b, 0, 0))],
        out_shape=[jax.ShapeDtypeStruct((nb, t, d), F32),
                   jax.ShapeDtypeStruct((nb, pad, 2 * d_ff), F32)],
        scratch_shapes=[pltpu.VMEM((pad, 2 * d_ff), F32), pltpu.VMEM((tm, d_ff), BF16)],
        compiler_params=_cparams(("parallel", "arbitrary")),
        name="conv_ffn",
    )(h, buf, gain.reshape(1, d), w_up, conv_w, w_down, final_gain.reshape(1, d))


def _xattn_kernel(x_ref, mk_ref, mv_ref, g_ref, wq_ref, wo_ref, o_ref, att, *, nb, tm):
    x = x_ref[...]
    xn = _rms(x, g_ref[...]).astype(BF16)
    q = _dot(xn, wq_ref[...])
    scale = XA_DH ** -0.5
    for b in range(nb):
        rows = slice(b * tm, (b + 1) * tm)
        for hh in range(XA_HEADS):
            cols = slice(hh * XA_DH, (hh + 1) * XA_DH)
            s = _dot_nt(q[rows, cols].astype(BF16), mk_ref[b, :, cols].astype(BF16)) * scale
            p = jnp.exp(s - jnp.max(s, axis=-1, keepdims=True))
            p = p / jnp.sum(p, axis=-1, keepdims=True)
            att[rows, cols] = _dot(p.astype(BF16), mv_ref[b, :, cols].astype(BF16))
    o_ref[...] = x + _dot(att[...].astype(BF16), wo_ref[...])


def mem_xattn(h, mem_k, mem_v, gain, w_q, w_o, *, nb, tm):
    r, d = h.shape
    n_seq, m, w = mem_k.shape
    t = r // n_seq
    assert t % tm == 0 and (nb == 1 or tm == t) and n_seq % nb == 0
    tiles = t // tm
    kern = functools.partial(_xattn_kernel, nb=nb, tm=tm)
    return pl.pallas_call(
        kern,
        grid=(n_seq // nb, tiles),
        in_specs=[pl.BlockSpec((nb * tm, d), lambda b, i: (b * tiles + i, 0)),
                  pl.BlockSpec((nb, m, w), lambda b, i: (b, 0, 0)),
                  pl.BlockSpec((nb, m, w), lambda b, i: (b, 0, 0)),
                  _resident((1, d)), _resident(w_q.shape), _resident(w_o.shape)],
        out_specs=pl.BlockSpec((nb * tm, d), lambda b, i: (b * tiles + i, 0)),
        out_shape=jax.ShapeDtypeStruct((r, d), F32),
        scratch_shapes=[pltpu.VMEM((nb * tm, w), F32)],
        compiler_params=_cparams(("parallel", "arbitrary")),
        name="mem_xattn",
    )(h, mem_k, mem_v, gain.reshape(1, d), w_q, w_o)


def _tri_inv(a, c):
    r = lax.broadcasted_iota(jnp.int32, (c, c), 0)
    q = lax.broadcasted_iota(jnp.int32, (c, c), 1)
    eye = (r == q).astype(F32)
    x = [eye - jnp.where((r >> 1) == (q >> 1), ai, 0.0) for ai in a]
    sh = 1
    while (1 << sh) < c:
        sh += 1
        mask = ((r >> sh) == (q >> sh)) & ((r >> (sh - 1)) != (q >> (sh - 1)))
        xb = [xi.astype(BF16) for xi in x]
        lx = [_dot(jnp.where(mask, ai, 0.0).astype(BF16), xbi) for ai, xbi in zip(a, xb)]
        x = [xi - _dot(xbi, li.astype(BF16)) for xi, xbi, li in zip(x, xb, lx)]
    return x


def _page_means_side_job(pt_ref, pool_hbm, km_ref, pbuf, psem, *, step, n_steps, layer, pps):
    slot = step % 2

    def copies(st, sl):
        return [pltpu.make_async_copy(pool_hbm.at[layer, pt_ref[st * pps + u]], pbuf.at[sl, u], psem.at[sl])
                for u in range(pps)]

    @pl.when(step == 0)
    def _():
        for cp in copies(step, slot):
            cp.start()

    @pl.when(step + 1 < n_steps)
    def _():
        for cp in copies(step + 1, 1 - slot):
            cp.start()

    for cp in copies(step, slot):
        cp.wait()
    for blk in range(pps // PPB):
        tot = jnp.sum(pbuf[slot, blk * PPB], axis=0)
        for w in range(1, PPB):
            tot = tot + jnp.sum(pbuf[slot, blk * PPB + w], axis=0)
        km_ref[0, blk] = tot * (1.0 / MOBA_BLOCK)


def _dn_kernel(*refs, tt, c, t_valid, nt, hps, pages):
    if pages is not None:
        pt_ref, refs = refs[0], refs[1:]
    (q_ref, k_ref, v_ref, z_ref, ba_ref, bq_ref, bk_ref, bv_ref, s0_ref,
     cwq_ref, cwk_ref, cwv_ref, avec_ref, dtb_ref, og_ref) = refs[:15]
    refs = refs[15:]
    if pages is not None:
        pool_hbm, o_ref, sfin_ref, km_ref, xq, xk, xv, state, pbuf, psem = refs
    else:
        o_ref, sfin_ref, xq, xk, xv, state = refs
    hg = pl.program_id(1)
    t = pl.program_id(2)
    lead = SUBLANES
    if pages is not None:
        step = (pl.program_id(0) * pl.num_programs(1) + hg) * nt + t
        _page_means_side_job(pt_ref, pool_hbm, km_ref, pbuf, psem, step=step, **pages)

    @pl.when(t == 0)
    def _():
        state[...] = s0_ref[0]
        xq[...] = bq_ref[0]
        xk[...] = bk_ref[0]
        xv[...] = bv_ref[0]

    def conv_silu(hist, x_ref, cw_ref):
        x = x_ref[0]
        taps = [x]
        for j in range(1, DN_CONV):
            taps.append(_shift_rows(taps[-1], hist[lead - j:lead - j + 1], 1))
        y = taps[DN_CONV - 1] * cw_ref[0:1]
        for j in range(1, DN_CONV):
            y = y + taps[DN_CONV - 1 - j] * cw_ref[j:j + 1]
        hist[...] = x[tt - lead:]
        return _silu(y)

    qa = conv_silu(xq, q_ref, cwq_ref)
    ka = conv_silu(xk, k_ref, cwk_ref)
    va = conv_silu(xv, v_ref, cwv_ref)

    ba = ba_ref[0]
    lane = lax.broadcasted_iota(jnp.int32, (tt, LANES), 1)
    beta_all = jax.nn.sigmoid(ba)
    g_all = -avec_ref[...] * jax.nn.softplus(ba + dtb_ref[...])
    if t_valid < nt * tt:
        row = t * tt + lax.broadcasted_iota(jnp.int32, (tt, LANES), 0)
        beta_all = jnp.where(row < t_valid, beta_all, 0.0)
        g_all = jnp.where(row < t_valid, g_all, 0.0)

    ri = lax.broadcasted_iota(jnp.int32, (c, c), 0)
    ci = lax.broadcasted_iota(jnp.int32, (c, c), 1)
    causal = ri >= ci
    strict = ri > ci
    ltri = causal.astype(BF16)
    ones8 = jnp.ones((SUBLANES, LANES), BF16)
    og = og_ref[...]

    q, k, v, beta, g_sel = [], [], [], [], []
    for hh in range(hps):
        cols = slice(hh * DN_DK, (hh + 1) * DN_DK)
        head = hg * hps + hh
        qh, kh = qa[:, cols], ka[:, cols]
        q.append(qh * lax.rsqrt(jnp.sum(qh * qh, axis=-1, keepdims=True) + EPS) * (DN_DK ** -0.5))
        k.append(kh * lax.rsqrt(jnp.sum(kh * kh, axis=-1, keepdims=True) + EPS))
        v.append(va[:, hh * DN_DV:(hh + 1) * DN_DV])
        beta.append(jnp.sum(jnp.where(lane == head, beta_all, 0.0), axis=1, keepdims=True))
        g_sel.append(jnp.where(lane == head + DN_HEADS, g_all, 0.0))

    n_ch = tt // c
    items = [(hh, slice(ch * c, (ch + 1) * c)) for ch in range(n_ch) for hh in range(hps)]
    gcs = [_dot_sel(ltri, g_sel[hh][sl]) for hh, sl in items]
    gcols = [jnp.sum(gc, axis=1, keepdims=True) for gc in gcs]
    grows = [_dot_sel(ones8, gc, nt=True)[0:1] for gc in gcs]
    egs = [jnp.exp(gcol) for gcol in gcols]
    kbs = [k[hh][sl] * beta[hh][sl] for hh, sl in items]
    kqs = [_dot_nt(jnp.concatenate([kb, q[hh][sl]], axis=0).astype(BF16), k[hh][sl].astype(BF16))
           for kb, (hh, sl) in zip(kbs, items)]
    gams = [jnp.exp(jnp.where(causal, gcol - grow, NEG)) for gcol, grow in zip(gcols, grows)]
    t_invs = _tri_inv([jnp.where(strict, kq[:c] * gam, 0.0) for kq, gam in zip(kqs, gams)], c)
    uws = [_dot(ti.astype(BF16),
                jnp.concatenate([v[hh][sl] * beta[hh][sl], kb * eg], axis=1).astype(BF16)).astype(BF16)
           for ti, (hh, sl), kb, eg in zip(t_invs, items, kbs, egs)]
    qkuws = [_dot((kq[c:] * gam).astype(BF16), uw) for kq, gam, uw in zip(kqs, gams, uws)]
    glasts = [gcol[c - 1:c] for gcol in gcols]
    kduws = [lax.dot_general((k[hh][sl] * jnp.exp(glast - gcol)).astype(BF16), uw, (((0,), (0,)), ((), ())),
                             preferred_element_type=F32)
             for (hh, sl), glast, gcol, uw in zip(items, glasts, gcols, uws)]
    lhss = [jnp.concatenate([q[hh][sl] * eg - qkuw[:, DN_DV:], kduw[:, DN_DV:]], axis=0).astype(BF16)
            for (hh, sl), eg, qkuw, kduw in zip(items, egs, qkuws, kduws)]

    s = [state[hh] for hh in range(hps)]
    for (hh, sl), lhs, qkuw, kduw, glast in zip(items, lhss, qkuws, kduws, glasts):
        r = _dot(lhs, s[hh].astype(BF16))
        o = r[:c] + qkuw[:, :DN_DV]
        s[hh] = s[hh] * jnp.exp(glast) - r[c:] + kduw[:, :DN_DV]
        o = o * lax.rsqrt(jnp.mean(o * o, axis=-1, keepdims=True) + EPS) * og
        cols = slice(hh * DN_DV, (hh + 1) * DN_DV)
        o_ref[0, sl, cols] = o * _silu(z_ref[0, sl, cols])
    for hh in range(hps):
        state[hh] = s[hh]

    @pl.when(t == nt - 1)
    def _():
        sfin_ref[0] = state[...]


def deltanet_core(qkv, z, ba, buf, s0, conv_w, a_log, dt_bias, o_gain, *, tt, c, t_valid, hps,
                  pool=None, pool_layer=0, page_table=None):
    b, tp, _ = qkv.shape
    hh = DN_HEADS
    nt = tp // tt
    assert tp % tt == 0 and tt % c == 0 and hh % hps == 0
    ng = hh // hps
    zpad = jnp.zeros((LANES - 2 * hh,), F32)
    avec = jnp.concatenate([jnp.zeros((hh,), F32), jnp.exp(a_log), zpad]).reshape(1, LANES)
    dtb = jnp.concatenate([jnp.zeros((hh,), F32), dt_bias, zpad]).reshape(1, LANES)
    wq, wv = hps * DN_DK, hps * DN_DV
    col = lambda part: pl.BlockSpec((1, tt, wq), lambda i, g, t, *_: (i, t, part * ng + g))
    bufc = lambda part: pl.BlockSpec((1, SUBLANES, wq), lambda i, g, t, *_: (i, 0, part * ng + g))
    cwc = lambda part: pl.BlockSpec((DN_CONV, wq), lambda i, g, t, *_: (0, part * ng + g))
    in_specs = [col(0), col(1), col(2),
                pl.BlockSpec((1, tt, wv), lambda i, g, t, *_: (i, t, g)),
                pl.BlockSpec((1, tt, LANES), lambda i, g, t, *_: (i, t, 0)),
                bufc(0), bufc(1), bufc(2),
                pl.BlockSpec((1, hps, DN_DK, DN_DV), lambda i, g, t, *_: (i, g, 0, 0)),
                cwc(0), cwc(1), cwc(2),
                pl.BlockSpec((1, LANES), lambda i, g, t, *_: (0, 0)),
                pl.BlockSpec((1, LANES), lambda i, g, t, *_: (0, 0)),
                pl.BlockSpec((1, DN_DV), lambda i, g, t, *_: (0, 0))]
    out_specs = [pl.BlockSpec((1, tt, wv), lambda i, g, t, *_: (i, t, g)),
                 pl.BlockSpec((1, hps, DN_DK, DN_DV), lambda i, g, t, *_: (i, g, 0, 0))]
    out_shape = [jax.ShapeDtypeStruct((b, tp, hh * DN_DV), F32),
                 jax.ShapeDtypeStruct((b, hh, DN_DK, DN_DV), F32)]
    scratch = [pltpu.VMEM((SUBLANES, wq), F32), pltpu.VMEM((SUBLANES, wq), F32),
               pltpu.VMEM((SUBLANES, wv), F32), pltpu.VMEM((hps, DN_DK, DN_DV), F32)]
    args = [qkv, qkv, qkv, z, ba, buf, buf, buf, s0, conv_w, conv_w, conv_w, avec, dtb,
            o_gain.reshape(1, DN_DV)]
    pages = None
    prefetch = []
    sem = ("parallel", "parallel", "arbitrary")
    if pool is not None:
        n_steps = b * ng * nt
        n_seq, n_pages = page_table.shape
        _, _, page, ph, pdh = pool.shape
        assert (n_seq * n_pages) % n_steps == 0
        pps = (n_seq * n_pages) // n_steps
        assert pps % PPB == 0 and n_pages % pps == 0
        pages = dict(n_steps=n_steps, layer=pool_layer, pps=pps)
        prefetch = [page_table.reshape(-1)]
        in_specs.append(pl.BlockSpec(memory_space=pl.ANY))
        args.append(pool)
        out_specs.append(pl.BlockSpec((1, pps // PPB, ph, pdh),
                                      lambda i, g, t, *_: ((i * ng + g) * nt + t, 0, 0, 0)))
        out_shape.append(jax.ShapeDtypeStruct((n_steps, pps // PPB, ph, pdh), F32))
        scratch += [pltpu.VMEM((2, pps, page, ph, pdh), F32), pltpu.SemaphoreType.DMA((2,))]
        sem = ("arbitrary", "arbitrary", "arbitrary")
    kern = functools.partial(_dn_kernel, tt=tt, c=c, t_valid=t_valid, nt=nt, hps=hps, pages=pages)
    res = pl.pallas_call(
        kern,
        grid_spec=pltpu.PrefetchScalarGridSpec(
            num_scalar_prefetch=len(prefetch), grid=(b, ng, nt),
            in_specs=in_specs, out_specs=out_specs, scratch_shapes=scratch),
        out_shape=out_shape,
        compiler_params=_cparams(sem),
        name="deltanet_core",
    )(*prefetch, *args)
    if pool is None:
        return res[0], res[1], None
    return res[0], res[1], res[2].reshape(n_seq, n_pages // PPB, ph, pdh)


def _topk_mask_rows(g, n_valid, n_rows, k_top):
    jrow = lax.broadcasted_iota(jnp.int32, g.shape, 0)
    cnt = jnp.zeros(g.shape, F32)
    for jp in range(min(n_rows, n_valid)):
        gb = g[jp:jp + 1, :]
        cnt = cnt + jnp.where(gb > g, 1.0, jnp.where(gb == g, jnp.where(jrow > jp, 1.0, 0.0), 0.0))
    return jnp.where(jrow < n_valid, cnt, float(k_top)) < float(k_top), cnt


LOG2E = 1.4426950408889634
N_EXT = 16


def _split3_f32(a):
    p1 = a.astype(BF16).astype(F32)
    r1 = a - p1
    p2 = r1.astype(BF16).astype(F32)
    return p1, p2, (r1 - p2).astype(BF16).astype(F32)


def _moba_kernel(q_ref, k_ref, v_ref, sl_ref, o_ref, kmean, kaug, vt, qtb_s, pen, sbuf, *, nb, nbp, grp, qb):
    i = pl.program_id(2)
    blk, dh = MOBA_BLOCK, MOBA_DH
    slope = sl_ref[0]

    @pl.when(i == 0)
    def _():
        kmean[...] = jnp.zeros_like(kmean)
        lane = lax.broadcasted_iota(jnp.int32, (blk, LANES), 1)
        crow = lax.broadcasted_iota(jnp.int32, (blk, LANES), 0).astype(F32)
        kext = jnp.where(lane < 3, crow, jnp.where(lane < 6, 1.0, 0.0)).astype(BF16)
        for j in range(nb):
            rows = slice(j * blk, (j + 1) * blk)
            kj = k_ref[0, rows, :]
            kmean[j:j + 1, :] = jnp.sum(kj, axis=0, keepdims=True) * (1.0 / blk)
            kaug[j] = jnp.concatenate([kj.astype(BF16), kext], axis=1)
            vt[j] = v_ref[0, rows, :].T.astype(BF16)
        km = kmean[...]
        jrow = lax.broadcasted_iota(jnp.int32, (nbp, blk), 0)
        ridx = lax.broadcasted_iota(jnp.int32, (nbp, blk), 1)
        for jq in range(nb):
            qt = (q_ref[0, jq * blk:(jq + 1) * blk, :] * (dh ** -0.5)).T
            qtb_s[jq] = (qt * LOG2E).astype(BF16)
            sel, _ = _topk_mask_rows(_dot3(km, qt), jq, nb, MOBA_TOPK)
            add = jnp.where(jrow == jq, 0.0, jnp.where(sel, 0.0, NEG))
            row_j = LOG2E * (add - slope * ((jq - jrow) * blk + ridx).astype(F32))
            for n, piece in enumerate(_split3_f32(row_j)):
                pen[n, jq] = piece

    qw = qb * blk
    iqs = [i * qb + n for n in range(qb)]
    cat = lambda xs: xs[0] if len(xs) == 1 else jnp.concatenate(xs, axis=1)
    qtb = cat([qtb_s[iq] for iq in iqs])
    slope_pieces = _split3_f32(jnp.concatenate([slope] * qb, axis=1) * LOG2E) if qb > 1 \
        else _split3_f32(slope * LOG2E)
    row16 = lax.broadcasted_iota(jnp.int32, (N_EXT, qw), 0)
    zpad = jnp.zeros((LANES - N_EXT, qw), BF16)

    def q_aug(j):
        pieces = [cat([pen[n, iq, pl.ds(j, 1), :] for iq in iqs]) for n in range(3)]
        ext = jnp.zeros((N_EXT, qw), F32)
        for n, piece in enumerate(tuple(slope_pieces) + tuple(pieces)):
            ext = jnp.where(row16 == n, piece, ext)
        return jnp.concatenate([qtb, ext.astype(BF16), zpad], axis=0)

    key = lax.broadcasted_iota(jnp.int32, (blk, qw), 0)
    col = lax.broadcasted_iota(jnp.int32, (blk, qw), 1)
    key_minus_qry = key - (col & (blk - 1))
    col_blk = lax.broadcasted_iota(jnp.int32, (1, qw), 1) // blk

    def fold8(x, op):
        return op(x.reshape(blk // SUBLANES, SUBLANES, qw), axis=0)

    n_trips = iqs[-1] // grp + 1

    def pass_scores(g, m8):
        for u in range(grp):
            j = g * grp + u
            s = _dot(kaug[j], q_aug(j))
            thr = jnp.full((1, qw), blk, jnp.int32)
            for n, iq in enumerate(iqs):
                thr = jnp.where((col_blk == n) & (j == iq), 0, thr)
            s = jnp.where(key_minus_qry > thr, NEG, s)
            sbuf[j] = s
            m8 = jnp.maximum(m8, fold8(s, jnp.max))
        return m8

    m8 = lax.fori_loop(0, n_trips, pass_scores, jnp.full((SUBLANES, qw), NEG, F32))
    m = jnp.max(m8, axis=0, keepdims=True)

    def pass_values(g, carry):
        l8, acc = carry
        for u in range(grp):
            j = g * grp + u
            p = jnp.exp2(sbuf[j] - m)
            l8 = l8 + fold8(p, jnp.sum)
            acc = acc + _dot(vt[j], p.astype(BF16))
        return l8, acc

    l8, acc = lax.fori_loop(0, n_trips, pass_values,
                            (jnp.zeros((SUBLANES, qw), F32), jnp.zeros((dh, qw), F32)))
    o_ref[0] = (acc / jnp.sum(l8, axis=0, keepdims=True)).T


def _alibi_rows(n_heads, width):
    slopes = jnp.exp2(-8.0 * jnp.arange(1, n_heads + 1, dtype=F32) / n_heads)
    return jnp.broadcast_to(slopes[:, None, None], (n_heads, 1, width))


def moba_prompt_attn(q, k, v):
    b, t, w = q.shape
    hh, dh, blk = MOBA_HEADS, MOBA_DH, MOBA_BLOCK
    assert t % blk == 0
    nb = t // blk
    nbp = -(-nb // SUBLANES) * SUBLANES
    grp = next(g for g in (4, 2, 1) if nb % g == 0)
    qb = 2 if nb % 2 == 0 else 1
    kern = functools.partial(_moba_kernel, nb=nb, nbp=nbp, grp=grp, qb=qb)
    return pl.pallas_call(
        kern,
        grid=(b, hh, nb // qb),
        in_specs=[pl.BlockSpec((1, t, dh), lambda bi, h, i: (bi, 0, h)),
                  pl.BlockSpec((1, t, dh), lambda bi, h, i: (bi, 0, h)),
                  pl.BlockSpec((1, t, dh), lambda bi, h, i: (bi, 0, h)),
                  pl.BlockSpec((1, 1, blk), lambda bi, h, i: (h, 0, 0))],
        out_specs=pl.BlockSpec((1, qb * blk, dh), lambda bi, h, i: (bi, i, h)),
        out_shape=jax.ShapeDtypeStruct((b, t, w), F32),
        scratch_shapes=[pltpu.VMEM((nbp, dh), F32), pltpu.VMEM((nb, blk, dh + LANES), BF16),
                        pltpu.VMEM((nb, dh, blk), BF16), pltpu.VMEM((nb, dh, blk), BF16),
                        pltpu.VMEM((3, nb, nbp, blk), F32), pltpu.VMEM((nb, blk, qb * blk), F32)],
        compiler_params=_cparams(("parallel", "parallel", "arbitrary")),
        name="moba_prompt_attn",
    )(q, k, v, _alibi_rows(hh, blk))


PAGES_PER_STEP = 16
PPB = MOBA_BLOCK // PAGE_SIZE


def _kmean_kernel(pt_ref, *refs):
    page_refs, o_ref = refs[:PAGES_PER_STEP], refs[PAGES_PER_STEP]
    for u in range(0, PAGES_PER_STEP, PPB):
        tot = jnp.sum(page_refs[u][...], axis=0)
        for w in range(1, PPB):
            tot = tot + jnp.sum(page_refs[u + w][...], axis=0)
        o_ref[0, u // PPB] = tot * (1.0 / MOBA_BLOCK)


def moba_block_means(pool_k, layer, page_table):
    n_seq, n_pages = page_table.shape
    _, _, page, hh, dh = pool_k.shape
    assert n_pages % PAGES_PER_STEP == 0 and PAGES_PER_STEP % PPB == 0
    steps = n_pages // PAGES_PER_STEP

    def page_spec(u):
        return pl.BlockSpec((None, None, page, hh, dh),
                            lambda b, g, pt: (layer, pt[b * n_pages + g * PAGES_PER_STEP + u], 0, 0, 0))

    bps = PAGES_PER_STEP // PPB
    return pl.pallas_call(
        _kmean_kernel,
        grid_spec=pltpu.PrefetchScalarGridSpec(
            num_scalar_prefetch=1, grid=(n_seq, steps),
            in_specs=[page_spec(u) for u in range(PAGES_PER_STEP)],
            out_specs=pl.BlockSpec((1, bps, hh, dh), lambda b, g, pt: (b, g, 0, 0))),
        out_shape=jax.ShapeDtypeStruct((n_seq, n_pages // PPB, hh, dh), F32),
        compiler_params=_cparams(("parallel", "arbitrary")),
        name="moba_block_means",
    )(page_table.reshape(-1), *([pool_k] * PAGES_PER_STEP))


def _moba_pick_kernel(q_ref, km_ref, o_ref, *, n_blk, t):
    hh, dh = MOBA_HEADS, MOBA_DH
    q = q_ref[0] * (dh ** -0.5)
    cols = []
    for h in range(hh):
        sl = slice(h * dh, (h + 1) * dh)
        cols.append(_dot3(km_ref[0, :, sl], q[:, sl], nt=True))
    gate = jnp.concatenate(cols, axis=1)
    _, cnt = _topk_mask_rows(gate, n_blk, n_blk, MOBA_TOPK)
    jrow = lax.broadcasted_iota(jnp.int32, gate.shape, 0).astype(F32)
    rows = [jnp.sum(jnp.where(cnt == float(r), jrow, 0.0), axis=0, keepdims=True)
            for r in range(MOBA_TOPK)]
    rows.append(jnp.zeros((SUBLANES - MOBA_TOPK, hh * t), F32))
    o_ref[0] = jnp.concatenate(rows, axis=0).astype(jnp.int32)


def moba_pick_blocks(q, kmean):
    n_seq, t, w = q.shape
    n_blk = kmean.shape[1]
    kern = functools.partial(_moba_pick_kernel, n_blk=n_blk, t=t)
    return pl.pallas_call(
        kern,
        grid=(n_seq,),
        in_specs=[pl.BlockSpec((1, t, w), lambda b: (b, 0, 0)),
                  pl.BlockSpec((1, n_blk, w), lambda b: (b, 0, 0))],
        out_specs=pl.BlockSpec((1, SUBLANES, MOBA_HEADS * t), lambda b: (b, 0, 0)),
        out_shape=jax.ShapeDtypeStruct((n_seq, SUBLANES, MOBA_HEADS * t), jnp.int32),
        compiler_params=_cparams(("parallel",)),
        name="moba_pick_blocks",
    )(q, kmean)


def _moba_dec_kernel(pt_ref, sel_ref, q_ref, kn_ref, vn_ref, sl_ref, pk_hbm, pv_hbm, o_ref,
                     kbuf, vbuf, sem, *, layer, t_len, past, n_pages, n_heads, hps, n_steps):
    n_buf = t_len * MOBA_TOPK * PPB
    n_groups = n_heads // hps
    b, hg = pl.program_id(0), pl.program_id(1)
    step = b * n_groups + hg
    slot = step % 2
    page = PAGE_SIZE
    dh = MOBA_DH

    def blk_of(bb, hh, t, r):
        return sel_ref[((bb * MOBA_TOPK + r) * n_heads + hh) * t_len + t]

    def copies(bb, grp, sl):
        out = []
        for hi in range(hps):
            hh = grp * hps + hi
            for t in range(t_len):
                for r in range(MOBA_TOPK):
                    for u in range(PPB):
                        phys = pt_ref[bb * n_pages + blk_of(bb, hh, t, r) * PPB + u]
                        i = hi * n_buf + (t * MOBA_TOPK + r) * PPB + u
                        out.append(pltpu.make_async_copy(pk_hbm.at[layer, phys, :, hh, :], kbuf.at[sl, i], sem.at[0, sl]))
                        out.append(pltpu.make_async_copy(pv_hbm.at[layer, phys, :, hh, :], vbuf.at[sl, i], sem.at[1, sl]))
        return out

    @pl.when(step == 0)
    def _():
        for cp in copies(b, hg, slot):
            cp.start()

    @pl.when(step + 1 < n_steps)
    def _():
        nxt = step + 1
        for cp in copies(nxt // n_groups, nxt % n_groups, 1 - slot):
            cp.start()

    row = lax.broadcasted_iota(jnp.int32, (t_len, page), 0)
    lane = lax.broadcasted_iota(jnp.int32, (t_len, page), 1)
    qi = lax.broadcasted_iota(jnp.int32, (t_len, t_len), 0)
    ki = lax.broadcasted_iota(jnp.int32, (t_len, t_len), 1)

    def tree(xs, op):
        while len(xs) > 1:
            xs = [op(xs[n], xs[n + 1]) for n in range(0, len(xs) - 1, 2)] + ([xs[-1]] if len(xs) % 2 else [])
        return xs[0]

    for i in range(hps * n_buf):
        pltpu.make_async_copy(pk_hbm.at[layer, 0, :, 0, :], kbuf.at[slot, i], sem.at[0, slot]).wait()
        pltpu.make_async_copy(pv_hbm.at[layer, 0, :, 0, :], vbuf.at[slot, i], sem.at[1, slot]).wait()

    his = range(hps)
    cols = [slice(hi * dh, (hi + 1) * dh) for hi in his]
    slopes = [sl_ref[hi][:, :page] for hi in his]
    qs = [(q_ref[0, :, cols[hi]] * (dh ** -0.5)).astype(BF16) for hi in his]
    s_owns = [jnp.where(ki <= qi, _dot_nt(qs[hi], kn_ref[0, :, cols[hi]].astype(BF16))
                        - slopes[hi][:, :t_len] * (qi - ki).astype(F32), NEG) for hi in his]
    scores = [[] for _ in his]
    for t in range(t_len):
        for r in range(MOBA_TOPK):
            for u in range(PPB):
                for hi in his:
                    dist = (past + row) - (blk_of(b, hg * hps + hi, t, r) * MOBA_BLOCK + u * page + lane)
                    s = _dot_nt(qs[hi], kbuf[slot, hi * n_buf + (t * MOBA_TOPK + r) * PPB + u].astype(BF16))
                    scores[hi].append(jnp.where(row == t, s - slopes[hi] * dist.astype(F32), NEG))
    ms = [jnp.maximum(s_owns[hi].max(axis=-1, keepdims=True),
                      tree(scores[hi], jnp.maximum).max(axis=-1, keepdims=True)) for hi in his]
    p_owns = [jnp.exp(s_owns[hi] - ms[hi]) for hi in his]
    ps = [[jnp.exp(s - ms[hi]) for s in scores[hi]] for hi in his]
    inv_ls = [1.0 / (p_owns[hi].sum(axis=-1, keepdims=True) + tree(ps[hi], jnp.add).sum(axis=-1, keepdims=True))
              for hi in his]
    accs = [_dot((p_owns[hi] * inv_ls[hi]).astype(BF16), vn_ref[0, :, cols[hi]].astype(BF16)) for hi in his]
    for i in range(n_buf):
        for hi in his:
            accs[hi] = accs[hi] + _dot((ps[hi][i] * inv_ls[hi]).astype(BF16),
                                       vbuf[slot, hi * n_buf + i].astype(BF16))
    for hi in his:
        o_ref[0, hi] = accs[hi]


def moba_decode_attn(q, k_new, v_new, pool_k, pool_v, layer, page_table, sel):
    n_seq, t_len, w = q.shape
    n_pages = page_table.shape[1]
    hh, dh = MOBA_HEADS, MOBA_DH
    _, _, page, _, _ = pool_k.shape
    past = n_pages * page
    assert past % MOBA_BLOCK == 0
    hps = 2 if hh % 2 == 0 else 1
    n_buf = hps * t_len * MOBA_TOPK * PPB
    row = lambda: pl.BlockSpec((1, t_len, hps * dh), lambda b, g, pt, sl: (b, 0, g))
    kern = functools.partial(_moba_dec_kernel, layer=layer, t_len=t_len, past=past, n_pages=n_pages,
                             n_heads=hh, hps=hps, n_steps=n_seq * (hh // hps))
    return pl.pallas_call(
        kern,
        grid_spec=pltpu.PrefetchScalarGridSpec(
            num_scalar_prefetch=2, grid=(n_seq, hh // hps),
            in_specs=[row(), row(), row(),
                      pl.BlockSpec((hps, 1, MOBA_BLOCK), lambda b, g, pt, sl: (g, 0, 0)),
                      pl.BlockSpec(memory_space=pl.ANY), pl.BlockSpec(memory_space=pl.ANY)],
            out_specs=pl.BlockSpec((1, hps, t_len, dh), lambda b, g, pt, sl: (b, g, 0, 0)),
            scratch_shapes=[pltpu.VMEM((2, n_buf, page, dh), F32), pltpu.VMEM((2, n_buf, page, dh), F32),
                            pltpu.SemaphoreType.DMA((2, 2))]),
        out_shape=jax.ShapeDtypeStruct((n_seq, hh, t_len, dh), F32),
        compiler_params=_cparams(("arbitrary", "arbitrary")),
        name="moba_decode_attn",
    )(page_table.reshape(-1), sel.reshape(-1), q, k_new, v_new, _alibi_rows(hh, MOBA_BLOCK), pool_k, pool_v)


def _last_rows(buf, x, n):
    t = x.shape[1]
    if t >= n:
        return x[:, t - n:]
    return jnp.concatenate([buf[:, buf.shape[1] - (n - t):], x], axis=1)


def _pad_front(buf, rows):
    return jnp.pad(buf, ((0, 0), (rows - buf.shape[1], 0), (0, 0)))


def kernel(x_prompt, x_sample, state_dn_s, state_dn_conv, cache_moba_k, cache_moba_v, page_table, cache_mem_k, cache_mem_v, state_ffn_conv, mem_prompt, norm_mix, norm_xattn, norm_mem, norm_ffn, norm_final, dn_w_in, dn_conv_w, dn_a_log, dn_dt_bias, dn_o_gain, dn_w_out, moba_w_qkv, moba_w_out, xa_w_q, xa_w_kv, xa_w_out, ffn_w_up, ffn_conv_w, ffn_w_down):
    bp, seq, d = x_prompt.shape
    bs, dseq, _ = x_sample.shape
    depth = norm_mix.shape[0]
    mem_len = mem_prompt.shape[1]
    d_ff = ffn_w_down.shape[1]
    qk_w = DN_HEADS * DN_DK
    v_w = DN_HEADS * DN_DV
    conv_ch = 2 * qk_w + v_w
    moba_w = MOBA_HEADS * MOBA_DH
    xa_w = XA_HEADS * XA_DH
    bf = lambda a: a.astype(BF16)

    hp = x_prompt.reshape(bp * seq, d)
    hs = x_sample.reshape(bs * dseq, d)
    outs = {k: [] for k in ("p_dn_s", "p_dn_c", "p_mk", "p_mv", "p_memk", "p_memv", "p_ffc",
                            "s_dn_s", "s_dn_c", "s_mk", "s_mv", "s_ffc")}
    kmeans = {}
    for layer in range(depth):
        if layer % 2 == 0:
            ia = layer // 2
            w_in = dn_w_in[ia]
            ws = [bf(w_in[:, :conv_ch]), bf(w_in[:, conv_ch:conv_ch + v_w]),
                  bf(jnp.pad(w_in[:, conv_ch + v_w:], ((0, 0), (0, LANES - 2 * DN_HEADS))))]
            w_out = bf(dn_w_out[ia])
            args = (dn_conv_w[ia], dn_a_log[ia], dn_dt_bias[ia], dn_o_gain[ia])
            qkv, z, ba = norm_proj(hp, norm_mix[layer], ws, tm=512)
            qkv3 = qkv.reshape(bp, seq, conv_ch)
            tt, hps = 512, 2
            n_steps = bp * (DN_HEADS // hps) * (seq // tt)
            n_pg_all = page_table.shape[0] * page_table.shape[1]
            ride = (layer + 1 < depth and n_pg_all % n_steps == 0 and (n_pg_all // n_steps) % PPB == 0
                    and page_table.shape[1] % (n_pg_all // n_steps) == 0)
            o, s_fin, km = deltanet_core(qkv3, z.reshape(bp, seq, v_w), ba.reshape(bp, seq, LANES),
                                         jnp.zeros((bp, SUBLANES, conv_ch), F32),
                                         jnp.zeros((bp, DN_HEADS, DN_DK, DN_DV), F32), *args,
                                         tt=tt, c=DN_CHUNK, t_valid=seq, hps=hps,
                                         **(dict(pool=cache_moba_k, pool_layer=ia, page_table=page_table)
                                            if ride else {}))
            kmeans[ia] = km
            hp = out_proj_res(o.reshape(bp * seq, v_w), w_out, hp, tm=512)
            outs["p_dn_s"].append(s_fin)
            outs["p_dn_c"].append(_last_rows(jnp.zeros((bp, DN_CONV - 1, conv_ch), F32), qkv3, DN_CONV - 1))
            qkv, z, ba = norm_proj(hs, norm_mix[layer], ws, tm=256)
            qkv3 = qkv.reshape(bs, dseq, conv_ch)
            tpad = 16
            padt = lambda a: jnp.pad(a.reshape(bs, dseq, -1), ((0, 0), (0, tpad - dseq), (0, 0)))
            o, s_fin, _ = deltanet_core(padt(qkv), padt(z), padt(ba),
                                     _pad_front(state_dn_conv[ia], SUBLANES), state_dn_s[ia], *args,
                                     tt=tpad, c=tpad, t_valid=dseq, hps=DN_HEADS)
            hs = out_proj_res(o[:, :dseq].reshape(bs * dseq, v_w), w_out, hs, tm=256)
            outs["s_dn_s"].append(s_fin)
            outs["s_dn_c"].append(_last_rows(state_dn_conv[ia], qkv3, DN_CONV - 1))
        else:
            ib = layer // 2
            w_qkv = moba_w_qkv[ib]
            ws = [bf(w_qkv[:, j * moba_w:(j + 1) * moba_w]) for j in range(3)]
            w_out = bf(moba_w_out[ib])
            q, k, v = (a.reshape(bp, seq, moba_w) for a in norm_proj(hp, norm_mix[layer], ws, tm=512))
            o = moba_prompt_attn(q, k, v)
            hp = out_proj_res(o.reshape(bp * seq, moba_w), w_out, hp, tm=512)
            outs["p_mk"].append(k.reshape(bp, seq, MOBA_HEADS, MOBA_DH))
            outs["p_mv"].append(v.reshape(bp, seq, MOBA_HEADS, MOBA_DH))
            q, k, v = (a.reshape(bs, dseq, moba_w) for a in norm_proj(hs, norm_mix[layer], ws, tm=256))
            kmean = kmeans.get(ib)
            if kmean is None:
                kmean = moba_block_means(cache_moba_k, ib, page_table)
            pick = moba_pick_blocks(q, kmean.reshape(bs, -1, moba_w))
            sel = pick[:, :MOBA_TOPK].reshape(bs, MOBA_TOPK, MOBA_HEADS, dseq)
            o = moba_decode_attn(q, k, v, cache_moba_k, cache_moba_v, ib, page_table, sel)
            o = o.transpose(0, 2, 1, 3).reshape(bs * dseq, moba_w)
            hs = out_proj_res(o, w_out, hs, tm=256)
            outs["s_mk"].append(k.reshape(bs, dseq, MOBA_HEADS, MOBA_DH))
            outs["s_mv"].append(v.reshape(bs, dseq, MOBA_HEADS, MOBA_DH))

        w_kv = xa_w_kv[layer]
        mk, mv = norm_proj(mem_prompt.reshape(bp * mem_len, d), norm_mem[layer],
                           [bf(w_kv[:, :xa_w]), bf(w_kv[:, xa_w:])], tm=256)
        mk = mk.reshape(bp, mem_len, xa_w)
        mv = mv.reshape(bp, mem_len, xa_w)
        outs["p_memk"].append(mk.reshape(bp, mem_len, XA_HEADS, XA_DH))
        outs["p_memv"].append(mv.reshape(bp, mem_len, XA_HEADS, XA_DH))
        w_q, w_o = bf(xa_w_q[layer]), bf(xa_w_out[layer])
        hp = mem_xattn(hp, mk, mv, norm_xattn[layer], w_q, w_o, nb=1, tm=512)
        hs = mem_xattn(hs, cache_mem_k[layer].reshape(bs, mem_len, xa_w),
                       cache_mem_v[layer].reshape(bs, mem_len, xa_w),
                       norm_xattn[layer], w_q, w_o, nb=8, tm=dseq)

        last = layer == depth - 1
        w_up, w_dn = bf(ffn_w_up[layer]), bf(ffn_w_down[layer])
        hp3, tail = conv_ffn(hp.reshape(bp, seq, d), jnp.zeros((bp, SUBLANES, 2 * d_ff), F32),
                             norm_ffn[layer], w_up, ffn_conv_w[layer], w_dn, norm_final,
                             tm=512, stride=1, final_norm=last)
        hp = hp3.reshape(bp * seq, d)
        outs["p_ffc"].append(tail[:, SUBLANES - (FFN_CONV - 1):])
        hs_tm = hs.reshape(bs, dseq, d).transpose(1, 0, 2).reshape(1, dseq * bs, d)
        buf_tm = state_ffn_conv[layer].transpose(1, 0, 2).reshape(1, (FFN_CONV - 1) * bs, 2 * d_ff)
        hs3, tail = conv_ffn(hs_tm, buf_tm, norm_ffn[layer], w_up, ffn_conv_w[layer], w_dn, norm_final,
                             tm=dseq * bs, stride=bs, final_norm=last)
        hs = hs3.reshape(dseq, bs, d).transpose(1, 0, 2).reshape(bs * dseq, d)
        outs["s_ffc"].append(tail.reshape(FFN_CONV - 1, bs, 2 * d_ff).transpose(1, 0, 2))

    y_prompt = hp.reshape(bp, seq, d)
    y_sample = hs.reshape(bs, dseq, d)
    st = lambda k: outs[k][0][None] if len(outs[k]) == 1 else jnp.stack(outs[k])
    return (y_prompt, y_sample, st("p_dn_s"), st("p_dn_c"), st("p_mk"), st("p_mv"),
            st("p_memk"), st("p_memv"), st("p_ffc"),
            st("s_dn_s"), st("s_dn_c"), st("s_mk"), st("s_mv"), st("s_ffc"))
```

```python
import functools

import jax
import jax.numpy as jnp
from jax import lax
from jax.experimental import pallas as pl
from jax.experimental.pallas import tpu as pltpu

F32 = jnp.float32
BF16 = jnp.bfloat16
EPS = 1e-6
NEG = -1e30

DN_HEADS = 8
DN_DK = 128
DN_DV = 128
DN_CONV = 4
DN_CHUNK = 64
MOBA_HEADS = 8
MOBA_DH = 128
MOBA_BLOCK = 256
MOBA_TOPK = 3
PAGE_SIZE = 128
XA_HEADS = 4
XA_DH = 128
FFN_CONV = 3

SUBLANES = 8
LANES = 128
VMEM_LIMIT = 56 * 1024 * 1024


def _cparams(sem):
    return pltpu.CompilerParams(dimension_semantics=sem, vmem_limit_bytes=VMEM_LIMIT)


def _resident(shape):
    nd = len(shape)
    return pl.BlockSpec(shape, lambda *_: (0,) * nd, pipeline_mode=pl.Buffered(1))


def _rms(x, gain):
    return x * lax.rsqrt(jnp.mean(x * x, axis=-1, keepdims=True) + EPS) * gain


def _silu(x):
    return x * jax.nn.sigmoid(x)


def _dot(a, b):
    return jnp.dot(a, b, preferred_element_type=F32)


def _dot_nt(a, b):
    return lax.dot_general(a, b, (((1,), (1,)), ((), ())), preferred_element_type=F32)


def _split2(a):
    hi = a.astype(BF16)
    lo = (a - hi.astype(F32)).astype(BF16)
    return hi, lo


def _dot3(a, b, nt=False):
    d = _dot_nt if nt else _dot
    ah, al = _split2(a)
    bh, bl = _split2(b)
    return d(ah, bh) + (d(al, bh) + d(ah, bl))


def _split3(a):
    p1 = a.astype(BF16)
    r1 = a - p1.astype(F32)
    p2 = r1.astype(BF16)
    r2 = r1 - p2.astype(F32)
    return p1, p2, r2.astype(BF16)


def _dot_sel(sel, x, nt=False):
    d = _dot_nt if nt else _dot
    p1, p2, p3 = _split3(x)
    return d(sel, p1) + (d(sel, p2) + d(sel, p3))


def _norm_proj_kernel(x_ref, g_ref, *refs, n_w):
    w_refs, o_refs = refs[:n_w], refs[n_w:]
    xn = _rms(x_ref[...], g_ref[...]).astype(BF16)
    for w_ref, o_ref in zip(w_refs, o_refs):
        n = w_ref.shape[1]
        for c in range(0, n, 512):
            cw = min(512, n - c)
            o_ref[:, c:c + cw] = _dot(xn, w_ref[:, c:c + cw])


def norm_proj(x, gain, ws, tm):
    r, d = x.shape
    tm = min(tm, r)
    assert r % tm == 0
    n_w = len(ws)
    return pl.pallas_call(
        functools.partial(_norm_proj_kernel, n_w=n_w),
        grid=(r // tm,),
        in_specs=[pl.BlockSpec((tm, d), lambda i: (i, 0)), _resident((1, d))]
        + [_resident(w.shape) for w in ws],
        out_specs=[pl.BlockSpec((tm, w.shape[1]), lambda i: (i, 0)) for w in ws],
        out_shape=[jax.ShapeDtypeStruct((r, w.shape[1]), F32) for w in ws],
        compiler_params=_cparams(("parallel",)),
        name="norm_proj",
    )(x, gain.reshape(1, d), *ws)


def _out_proj_kernel(a_ref, w_ref, h_ref, o_ref):
    o_ref[...] = h_ref[...] + _dot(a_ref[...].astype(BF16), w_ref[...])


def out_proj_res(a, w, h, tm):
    r, k = a.shape
    d = w.shape[1]
    tm = min(tm, r)
    assert r % tm == 0
    return pl.pallas_call(
        _out_proj_kernel,
        grid=(r // tm,),
        in_specs=[pl.BlockSpec((tm, k), lambda i: (i, 0)), _resident(w.shape),
                  pl.BlockSpec((tm, d), lambda i: (i, 0))],
        out_specs=pl.BlockSpec((tm, d), lambda i: (i, 0)),
        out_shape=jax.ShapeDtypeStruct((r, d), F32),
        compiler_params=_cparams(("parallel",)),
        name="out_proj_res",
    )(a, w, h)


def _shift_rows(u, head, stride):
    rows = u.shape[0]
    if stride % SUBLANES == 0:
        return jnp.concatenate([head, u[:rows - stride]], axis=0)
    assert stride == 1
    rolled = pltpu.roll(u, 1, axis=0)
    first = jnp.where(lax.broadcasted_iota(jnp.int32, (SUBLANES, u.shape[1]), 0) == 0, head, rolled[:SUBLANES])
    return jnp.concatenate([first, rolled[SUBLANES:]], axis=0)


def _ffn_kernel(x_ref, buf_ref, g_ref, wup_ref, cw_ref, wdn_ref, fg_ref, o_ref, tail_ref, carry, hid,
                *, tm, stride, pad, d_ff, final_norm, chunk):
    t = pl.program_id(1)

    @pl.when(t == 0)
    def _():
        carry[...] = buf_ref[0]

    x = x_ref[0]
    xn = _rms(x, g_ref[...]).astype(BF16)
    for j0 in range(0, d_ff, chunk):
        cs = []
        for c0 in (j0, d_ff + j0):
            cols = slice(c0, c0 + chunk)
            u = _dot(xn, wup_ref[:, cols])
            prev = carry[:, cols]
            u1 = _shift_rows(u, prev[pad - stride:], stride)
            u2 = _shift_rows(u1, prev[pad - 2 * stride:pad - stride], stride)
            cs.append(u2 * cw_ref[0:1, cols] + u1 * cw_ref[1:2, cols] + u * cw_ref[2:3, cols])
            carry[:, cols] = u[tm - pad:]
        hid[:, j0:j0 + chunk] = (_silu(cs[0]) * cs[1]).astype(BF16)
    out = x + _dot(hid[...], wdn_ref[...])
    if final_norm:
        out = _rms(out, fg_ref[...])
    o_ref[0] = out
    tail_ref[0] = carry[...]


def conv_ffn(h, buf, gain, w_up, conv_w, w_down, final_gain, *, tm, stride, final_norm):
    nb, t, d = h.shape
    pad = buf.shape[1]
    d_ff = w_down.shape[0]
    tm = min(tm, t)
    assert t % tm == 0 and pad % SUBLANES == 0 and pad >= 2 * stride and tm >= pad
    kern = functools.partial(_ffn_kernel, tm=tm, stride=stride, pad=pad, d_ff=d_ff,
                             final_norm=final_norm, chunk=256)
    return pl.pallas_call(
        kern,
        grid=(nb, t // tm),
        in_specs=[pl.BlockSpec((1, tm, d), lambda b, i: (b, i, 0)),
                  pl.BlockSpec((1, pad, 2 * d_ff), lambda b, i: (b, 0, 0)),
                  _resident((1, d)), _resident(w_up.shape), _resident(conv_w.shape),
                  _resident(w_down.shape), _resident((1, d))],
        out_specs=[pl.BlockSpec((1, tm, d), lambda b, i: (b, i, 0)),
                   pl.BlockSpec((1, pad, 2 * d_ff), lambda b, i: (b, 0, 0))],
        out_shape=[jax.ShapeDtypeStruct((nb, t, d), F32),
                   jax.ShapeDtypeStruct((nb, pad, 2 * d_ff), F32)],
        scratch_shapes=[pltpu.VMEM((pad, 2 * d_ff), F32), pltpu.VMEM((tm, d_ff), BF16)],
        compiler_params=_cparams(("parallel", "arbitrary")),
        name="conv_ffn",
    )(h, buf, gain.reshape(1, d), w_up, conv_w, w_down, final_gain.reshape(1, d))


def _xattn_kernel(x_ref, mk_ref, mv_ref, g_ref, wq_ref, wo_ref, o_ref, att, *, nb, tm):
    x = x_ref[...]
    xn = _rms(x, g_ref[...]).astype(BF16)
    q = _dot(xn, wq_ref[...])
    scale = XA_DH ** -0.5
    for b in range(nb):
        rows = slice(b * tm, (b + 1) * tm)
        for hh in range(XA_HEADS):
            cols = slice(hh * XA_DH, (hh + 1) * XA_DH)
            s = _dot_nt(q[rows, cols].astype(BF16), mk_ref[b, :, cols].astype(BF16)) * scale
            p = jnp.exp(s - jnp.max(s, axis=-1, keepdims=True))
            p = p / jnp.sum(p, axis=-1, keepdims=True)
            att[rows, cols] = _dot(p.astype(BF16), mv_ref[b, :, cols].astype(BF16))
    o_ref[...] = x + _dot(att[...].astype(BF16), wo_ref[...])


def mem_xattn(h, mem_k, mem_v, gain, w_q, w_o, *, nb, tm):
    r, d = h.shape
    n_seq, m, w = mem_k.shape
    t = r // n_seq
    assert t % tm == 0 and (nb == 1 or tm == t) and n_seq % nb == 0
    tiles = t // tm
    kern = functools.partial(_xattn_kernel, nb=nb, tm=tm)
    return pl.pallas_call(
        kern,
        grid=(n_seq // nb, tiles),
        in_specs=[pl.BlockSpec((nb * tm, d), lambda b, i: (b * tiles + i, 0)),
                  pl.BlockSpec((nb, m, w), lambda b, i: (b, 0, 0)),
                  pl.BlockSpec((nb, m, w), lambda b, i: (b, 0, 0)),
                  _resident((1, d)), _resident(w_q.shape), _resident(w_o.shape)],
        out_specs=pl.BlockSpec((nb * tm, d), lambda b, i: (b * tiles + i, 0)),
        out_shape=jax.ShapeDtypeStruct((r, d), F32),
        scratch_shapes=[pltpu.VMEM((nb * tm, w), F32)],
        compiler_params=_cparams(("parallel", "arbitrary")),
        name="mem_xattn",
    )(h, mem_k, mem_v, gain.reshape(1, d), w_q, w_o)


def _tri_inv(a, c):
    r = lax.broadcasted_iota(jnp.int32, (c, c), 0)
    q = lax.broadcasted_iota(jnp.int32, (c, c), 1)
    eye = (r == q).astype(F32)
    x = [eye - jnp.where((r >> 1) == (q >> 1), ai, 0.0) for ai in a]
    sh = 1
    while (1 << sh) < c:
        sh += 1
        mask = ((r >> sh) == (q >> sh)) & ((r >> (sh - 1)) != (q >> (sh - 1)))
        xb = [xi.astype(BF16) for xi in x]
        lx = [_dot(jnp.where(mask, ai, 0.0).astype(BF16), xbi) for ai, xbi in zip(a, xb)]
        x = [xi - _dot(xbi, li.astype(BF16)) for xi, xbi, li in zip(x, xb, lx)]
    return x


def _page_means_side_job(pt_ref, pool_hbm, km_ref, pbuf, psem, *, step, n_steps, layer, pps):
    slot = step % 2

    def copies(st, sl):
        return [pltpu.make_async_copy(pool_hbm.at[layer, pt_ref[st * pps + u]], pbuf.at[sl, u], psem.at[sl])
                for u in range(pps)]

    @pl.when(step == 0)
    def _():
        for cp in copies(step, slot):
            cp.start()

    @pl.when(step + 1 < n_steps)
    def _():
        for cp in copies(step + 1, 1 - slot):
            cp.start()

    for cp in copies(step, slot):
        cp.wait()
    for blk in range(pps // PPB):
        tot = jnp.sum(pbuf[slot, blk * PPB], axis=0)
        for w in range(1, PPB):
            tot = tot + jnp.sum(pbuf[slot, blk * PPB + w], axis=0)
        km_ref[0, blk] = tot * (1.0 / MOBA_BLOCK)


def _dn_kernel(*refs, tt, c, t_valid, nt, hps, pages):
    if pages is not None:
        pt_ref, refs = refs[0], refs[1:]
    (q_ref, k_ref, v_ref, z_ref, ba_ref, bq_ref, bk_ref, bv_ref, s0_ref,
     cwq_ref, cwk_ref, cwv_ref, avec_ref, dtb_ref, og_ref) = refs[:15]
    refs = refs[15:]
    if pages is not None:
        pool_hbm, o_ref, sfin_ref, km_ref, xq, xk, xv, state, pbuf, psem = refs
    else:
        o_ref, sfin_ref, xq, xk, xv, state = refs
    hg = pl.program_id(1)
    t = pl.program_id(2)
    lead = SUBLANES
    if pages is not None:
        step = (pl.program_id(0) * pl.num_programs(1) + hg) * nt + t
        _page_means_side_job(pt_ref, pool_hbm, km_ref, pbuf, psem, step=step, **pages)

    @pl.when(t == 0)
    def _():
        state[...] = s0_ref[0]
        xq[...] = bq_ref[0]
        xk[...] = bk_ref[0]
        xv[...] = bv_ref[0]

    def conv_silu(hist, x_ref, cw_ref):
        x = x_ref[0]
        taps = [x]
        for j in range(1, DN_CONV):
            taps.append(_shift_rows(taps[-1], hist[lead - j:lead - j + 1], 1))
        y = taps[DN_CONV - 1] * cw_ref[0:1]
        for j in range(1, DN_CONV):
            y = y + taps[DN_CONV - 1 - j] * cw_ref[j:j + 1]
        hist[...] = x[tt - lead:]
        return _silu(y)

    qa = conv_silu(xq, q_ref, cwq_ref)
    ka = conv_silu(xk, k_ref, cwk_ref)
    va = conv_silu(xv, v_ref, cwv_ref)

    ba = ba_ref[0]
    lane = lax.broadcasted_iota(jnp.int32, (tt, LANES), 1)
    beta_all = jax.nn.sigmoid(ba)
    g_all = -avec_ref[...] * jax.nn.softplus(ba + dtb_ref[...])
    if t_valid < nt * tt:
        row = t * tt + lax.broadcasted_iota(jnp.int32, (tt, LANES), 0)
        beta_all = jnp.where(row < t_valid, beta_all, 0.0)
        g_all = jnp.where(row < t_valid, g_all, 0.0)

    ri = lax.broadcasted_iota(jnp.int32, (c, c), 0)
    ci = lax.broadcasted_iota(jnp.int32, (c, c), 1)
    causal = ri >= ci
    strict = ri > ci
    ltri = causal.astype(BF16)
    ones8 = jnp.ones((SUBLANES, LANES), BF16)
    og = og_ref[...]

    q, k, v, beta, g_sel = [], [], [], [], []
    for hh in range(hps):
        cols = slice(hh * DN_DK, (hh + 1) * DN_DK)
        head = hg * hps + hh
        qh, kh = qa[:, cols], ka[:, cols]
        q.append(qh * lax.rsqrt(jnp.sum(qh * qh, axis=-1, keepdims=True) + EPS) * (DN_DK ** -0.5))
        k.append(kh * lax.rsqrt(jnp.sum(kh * kh, axis=-1, keepdims=True) + EPS))
        v.append(va[:, hh * DN_DV:(hh + 1) * DN_DV])
        beta.append(jnp.sum(jnp.where(lane == head, beta_all, 0.0), axis=1, keepdims=True))
        g_sel.append(jnp.where(lane == head + DN_HEADS, g_all, 0.0))

    n_ch = tt // c
    items = [(hh, slice(ch * c, (ch + 1) * c)) for ch in range(n_ch) for hh in range(hps)]
    gcs = [_dot_sel(ltri, g_sel[hh][sl]) for hh, sl in items]
    gcols = [jnp.sum(gc, axis=1, keepdims=True) for gc in gcs]
    grows = [_dot_sel(ones8, gc, nt=True)[0:1] for gc in gcs]
    egs = [jnp.exp(gcol) for gcol in gcols]
    kbs = [k[hh][sl] * beta[hh][sl] for hh, sl in items]
    kqs = [_dot_nt(jnp.concatenate([kb, q[hh][sl]], axis=0).astype(BF16), k[hh][sl].astype(BF16))
           for kb, (hh, sl) in zip(kbs, items)]
    gams = [jnp.exp(jnp.where(causal, gcol - grow, NEG)) for gcol, grow in zip(gcols, grows)]
    t_invs = _tri_inv([jnp.where(strict, kq[:c] * gam, 0.0) for kq, gam in zip(kqs, gams)], c)
    uws = [_dot(ti.astype(BF16),
                jnp.concatenate([v[hh][sl] * beta[hh][sl], kb * eg], axis=1).astype(BF16)).astype(BF16)
           for ti, (hh, sl), kb, eg in zip(t_invs, items, kbs, egs)]
    qkuws = [_dot((kq[c:] * gam).astype(BF16), uw) for kq, gam, uw in zip(kqs, gams, uws)]
    glasts = [gcol[c - 1:c] for gcol in gcols]
    kduws = [lax.dot_general((k[hh][sl] * jnp.exp(glast - gcol)).astype(BF16), uw, (((0,), (0,)), ((), ())),
                             preferred_element_type=F32)
             for (hh, sl), glast, gcol, uw in zip(items, glasts, gcols, uws)]
    lhss = [jnp.concatenate([q[hh][sl] * eg - qkuw[:, DN_DV:], kduw[:, DN_DV:]], axis=0).astype(BF16)
            for (hh, sl), eg, qkuw, kduw in zip(items, egs, qkuws, kduws)]

    s = [state[hh] for hh in range(hps)]
    for (hh, sl), lhs, qkuw, kduw, glast in zip(items, lhss, qkuws, kduws, glasts):
        r = _dot(lhs, s[hh].astype(BF16))
        o = r[:c] + qkuw[:, :DN_DV]
        s[hh] = s[hh] * jnp.exp(glast) - r[c:] + kduw[:, :DN_DV]
        o = o * lax.rsqrt(jnp.mean(o * o, axis=-1, keepdims=True) + EPS) * og
        cols = slice(hh * DN_DV, (hh + 1) * DN_DV)
        o_ref[0, sl, cols] = o * _silu(z_ref[0, sl, cols])
    for hh in range(hps):
        state[hh] = s[hh]

    @pl.when(t == nt - 1)
    def _():
        sfin_ref[0] = state[...]


def deltanet_core(qkv, z, ba, buf, s0, conv_w, a_log, dt_bias, o_gain, *, tt, c, t_valid, hps,
                  pool=None, pool_layer=0, page_table=None):
    b, tp, _ = qkv.shape
    hh = DN_HEADS
    nt = tp // tt
    assert tp % tt == 0 and tt % c == 0 and hh % hps == 0
    ng = hh // hps
    zpad = jnp.zeros((LANES - 2 * hh,), F32)
    avec = jnp.concatenate([jnp.zeros((hh,), F32), jnp.exp(a_log), zpad]).reshape(1, LANES)
    dtb = jnp.concatenate([jnp.zeros((hh,), F32), dt_bias, zpad]).reshape(1, LANES)
    wq, wv = hps * DN_DK, hps * DN_DV
    col = lambda part: pl.BlockSpec((1, tt, wq), lambda i, g, t, *_: (i, t, part * ng + g))
    bufc = lambda part: pl.BlockSpec((1, SUBLANES, wq), lambda i, g, t, *_: (i, 0, part * ng + g))
    cwc = lambda part: pl.BlockSpec((DN_CONV, wq), lambda i, g, t, *_: (0, part * ng + g))
    in_specs = [col(0), col(1), col(2),
                pl.BlockSpec((1, tt, wv), lambda i, g, t, *_: (i, t, g)),
                pl.BlockSpec((1, tt, LANES), lambda i, g, t, *_: (i, t, 0)),
                bufc(0), bufc(1), bufc(2),
                pl.BlockSpec((1, hps, DN_DK, DN_DV), lambda i, g, t, *_: (i, g, 0, 0)),
                cwc(0), cwc(1), cwc(2),
                pl.BlockSpec((1, LANES), lambda i, g, t, *_: (0, 0)),
                pl.BlockSpec((1, LANES), lambda i, g, t, *_: (0, 0)),
                pl.BlockSpec((1, DN_DV), lambda i, g, t, *_: (0, 0))]
    out_specs = [pl.BlockSpec((1, tt, wv), lambda i, g, t, *_: (i, t, g)),
                 pl.BlockSpec((1, hps, DN_DK, DN_DV), lambda i, g, t, *_: (i, g, 0, 0))]
    out_shape = [jax.ShapeDtypeStruct((b, tp, hh * DN_DV), F32),
                 jax.ShapeDtypeStruct((b, hh, DN_DK, DN_DV), F32)]
    scratch = [pltpu.VMEM((SUBLANES, wq), F32), pltpu.VMEM((SUBLANES, wq), F32),
               pltpu.VMEM((SUBLANES, wv), F32), pltpu.VMEM((hps, DN_DK, DN_DV), F32)]
    args = [qkv, qkv, qkv, z, ba, buf, buf, buf, s0, conv_w, conv_w, conv_w, avec, dtb,
            o_gain.reshape(1, DN_DV)]
    pages = None
    prefetch = []
    sem = ("parallel", "parallel", "arbitrary")
    if pool is not None:
        n_steps = b * ng * nt
        n_seq, n_pages = page_table.shape
        _, _, page, ph, pdh = pool.shape
        assert (n_seq * n_pages) % n_steps == 0
        pps = (n_seq * n_pages) // n_steps
        assert pps % PPB == 0 and n_pages % pps == 0
        pages = dict(n_steps=n_steps, layer=pool_layer, pps=pps)
        prefetch = [page_table.reshape(-1)]
        in_specs.append(pl.BlockSpec(memory_space=pl.ANY))
        args.append(pool)
        out_specs.append(pl.BlockSpec((1, pps // PPB, ph, pdh),
                                      lambda i, g, t, *_: ((i * ng + g) * nt + t, 0, 0, 0)))
        out_shape.append(jax.ShapeDtypeStruct((n_steps, pps // PPB, ph, pdh), F32))
        scratch += [pltpu.VMEM((2, pps, page, ph, pdh), F32), pltpu.SemaphoreType.DMA((2,))]
        sem = ("arbitrary", "arbitrary", "arbitrary")
    kern = functools.partial(_dn_kernel, tt=tt, c=c, t_valid=t_valid, nt=nt, hps=hps, pages=pages)
    res = pl.pallas_call(
        kern,
        grid_spec=pltpu.PrefetchScalarGridSpec(
            num_scalar_prefetch=len(prefetch), grid=(b, ng, nt),
            in_specs=in_specs, out_specs=out_specs, scratch_shapes=scratch),
        out_shape=out_shape,
        compiler_params=_cparams(sem),
        name="deltanet_core",
    )(*prefetch, *args)
    if pool is None:
        return res[0], res[1], None
    return res[0], res[1], res[2].reshape(n_seq, n_pages // PPB, ph, pdh)


def _topk_mask_rows(g, n_valid, n_rows, k_top):
    jrow = lax.broadcasted_iota(jnp.int32, g.shape, 0)
    cnt = jnp.zeros(g.shape, F32)
    for jp in range(min(n_rows, n_valid)):
        gb = g[jp:jp + 1, :]
        cnt = cnt + jnp.where(gb > g, 1.0, jnp.where(gb == g, jnp.where(jrow > jp, 1.0, 0.0), 0.0))
    return jnp.where(jrow < n_valid, cnt, float(k_top)) < float(k_top), cnt


LOG2E = 1.4426950408889634
N_EXT = 16


def _split3_f32(a):
    p1 = a.astype(BF16).astype(F32)
    r1 = a - p1
    p2 = r1.astype(BF16).astype(F32)
    return p1, p2, (r1 - p2).astype(BF16).astype(F32)


def _moba_kernel(q_ref, k_ref, v_ref, sl_ref, o_ref, kmean, kaug, vt, qtb_s, pen, sbuf, *, nb, nbp, grp, qb):
    i = pl.program_id(2)
    blk, dh = MOBA_BLOCK, MOBA_DH
    slope = sl_ref[0]

    @pl.when(i == 0)
    def _():
        kmean[...] = jnp.zeros_like(kmean)
        lane = lax.broadcasted_iota(jnp.int32, (blk, LANES), 1)
        crow = lax.broadcasted_iota(jnp.int32, (blk, LANES), 0).astype(F32)
        kext = jnp.where(lane < 3, crow, jnp.where(lane < 6, 1.0, 0.0)).astype(BF16)
        for j in range(nb):
            rows = slice(j * blk, (j + 1) * blk)
            kj = k_ref[0, rows, :]
            kmean[j:j + 1, :] = jnp.sum(kj, axis=0, keepdims=True) * (1.0 / blk)
            kaug[j] = jnp.concatenate([kj.astype(BF16), kext], axis=1)
            vt[j] = v_ref[0, rows, :].T.astype(BF16)
        km = kmean[...]
        jrow = lax.broadcasted_iota(jnp.int32, (nbp, blk), 0)
        ridx = lax.broadcasted_iota(jnp.int32, (nbp, blk), 1)
        for jq in range(nb):
            qt = (q_ref[0, jq * blk:(jq + 1) * blk, :] * (dh ** -0.5)).T
            qtb_s[jq] = (qt * LOG2E).astype(BF16)
            sel, _ = _topk_mask_rows(_dot3(km, qt), jq, nb, MOBA_TOPK)
            add = jnp.where(jrow == jq, 0.0, jnp.where(sel, 0.0, NEG))
            row_j = LOG2E * (add - slope * ((jq - jrow) * blk + ridx).astype(F32))
            for n, piece in enumerate(_split3_f32(row_j)):
                pen[n, jq] = piece

    qw = qb * blk
    iqs = [i * qb + n for n in range(qb)]
    cat = lambda xs: xs[0] if len(xs) == 1 else jnp.concatenate(xs, axis=1)
    qtb = cat([qtb_s[iq] for iq in iqs])
    slope_pieces = _split3_f32(jnp.concatenate([slope] * qb, axis=1) * LOG2E) if qb > 1 \
        else _split3_f32(slope * LOG2E)
    row16 = lax.broadcasted_iota(jnp.int32, (N_EXT, qw), 0)
    zpad = jnp.zeros((LANES - N_EXT, qw), BF16)

    def q_aug(j):
        pieces = [cat([pen[n, iq, pl.ds(j, 1), :] for iq in iqs]) for n in range(3)]
        ext = jnp.zeros((N_EXT, qw), F32)
        for n, piece in enumerate(tuple(slope_pieces) + tuple(pieces)):
            ext = jnp.where(row16 == n, piece, ext)
        return jnp.concatenate([qtb, ext.astype(BF16), zpad], axis=0)

    key = lax.broadcasted_iota(jnp.int32, (blk, qw), 0)
    col = lax.broadcasted_iota(jnp.int32, (blk, qw), 1)
    key_minus_qry = key - (col & (blk - 1))
    col_blk = lax.broadcasted_iota(jnp.int32, (1, qw), 1) // blk

    def fold8(x, op):
        return op(x.reshape(blk // SUBLANES, SUBLANES, qw), axis=0)

    n_trips = iqs[-1] // grp + 1

    def pass_scores(g, m8):
        for u in range(grp):
            j = g * grp + u
            s = _dot(kaug[j], q_aug(j))
            thr = jnp.full((1, qw), blk, jnp.int32)
            for n, iq in enumerate(iqs):
                thr = jnp.where((col_blk == n) & (j == iq), 0, thr)
            s = jnp.where(key_minus_qry > thr, NEG, s)
            sbuf[j] = s
            m8 = jnp.maximum(m8, fold8(s, jnp.max))
        return m8

    m8 = lax.fori_loop(0, n_trips, pass_scores, jnp.full((SUBLANES, qw), NEG, F32))
    m = jnp.max(m8, axis=0, keepdims=True)

    def pass_values(g, carry):
        l8, acc = carry
        for u in range(grp):
            j = g * grp + u
            p = jnp.exp2(sbuf[j] - m)
            l8 = l8 + fold8(p, jnp.sum)
            acc = acc + _dot(vt[j], p.astype(BF16))
        return l8, acc

    l8, acc = lax.fori_loop(0, n_trips, pass_values,
                            (jnp.zeros((SUBLANES, qw), F32), jnp.zeros((dh, qw), F32)))
    o_ref[0] = (acc / jnp.sum(l8, axis=0, keepdims=True)).T


def _alibi_rows(n_heads, width):
    slopes = jnp.exp2(-8.0 * jnp.arange(1, n_heads + 1, dtype=F32) / n_heads)
    return jnp.broadcast_to(slopes[:, None, None], (n_heads, 1, width))


def moba_prompt_attn(q, k, v):
    b, t, w = q.shape
    hh, dh, blk = MOBA_HEADS, MOBA_DH, MOBA_BLOCK
    assert t % blk == 0
    nb = t // blk
    nbp = -(-nb // SUBLANES) * SUBLANES
    grp = next(g for g in (4, 2, 1) if nb % g == 0)
    qb = next(g for g in (4, 2, 1) if nb % g == 0)
    kern = functools.partial(_moba_kernel, nb=nb, nbp=nbp, grp=grp, qb=qb)
    return pl.pallas_call(
        kern,
        grid=(b, hh, nb // qb),
        in_specs=[pl.BlockSpec((1, t, dh), lambda bi, h, i: (bi, 0, h)),
                  pl.BlockSpec((1, t, dh), lambda bi, h, i: (bi, 0, h)),
                  pl.BlockSpec((1, t, dh), lambda bi, h, i: (bi, 0, h)),
                  pl.BlockSpec((1, 1, blk), lambda bi, h, i: (h, 0, 0))],
        out_specs=pl.BlockSpec((1, qb * blk, dh), lambda bi, h, i: (bi, i, h)),
        out_shape=jax.ShapeDtypeStruct((b, t, w), F32),
        scratch_shapes=[pltpu.VMEM((nbp, dh), F32), pltpu.VMEM((nb, blk, dh + LANES), BF16),
                        pltpu.VMEM((nb, dh, blk), BF16), pltpu.VMEM((nb, dh, blk), BF16),
                        pltpu.VMEM((3, nb, nbp, blk), F32), pltpu.VMEM((nb, blk, qb * blk), F32)],
        compiler_params=_cparams(("parallel", "parallel", "arbitrary")),
        name="moba_prompt_attn",
    )(q, k, v, _alibi_rows(hh, blk))


PAGES_PER_STEP = 16
PPB = MOBA_BLOCK // PAGE_SIZE


def _kmean_kernel(pt_ref, *refs):
    page_refs, o_ref = refs[:PAGES_PER_STEP], refs[PAGES_PER_STEP]
    for u in range(0, PAGES_PER_STEP, PPB):
        tot = jnp.sum(page_refs[u][...], axis=0)
        for w in range(1, PPB):
            tot = tot + jnp.sum(page_refs[u + w][...], axis=0)
        o_ref[0, u // PPB] = tot * (1.0 / MOBA_BLOCK)


def moba_block_means(pool_k, layer, page_table):
    n_seq, n_pages = page_table.shape
    _, _, page, hh, dh = pool_k.shape
    assert n_pages % PAGES_PER_STEP == 0 and PAGES_PER_STEP % PPB == 0
    steps = n_pages // PAGES_PER_STEP

    def page_spec(u):
        return pl.BlockSpec((None, None, page, hh, dh),
                            lambda b, g, pt: (layer, pt[b * n_pages + g * PAGES_PER_STEP + u], 0, 0, 0))

    bps = PAGES_PER_STEP // PPB
    return pl.pallas_call(
        _kmean_kernel,
        grid_spec=pltpu.PrefetchScalarGridSpec(
            num_scalar_prefetch=1, grid=(n_seq, steps),
            in_specs=[page_spec(u) for u in range(PAGES_PER_STEP)],
            out_specs=pl.BlockSpec((1, bps, hh, dh), lambda b, g, pt: (b, g, 0, 0))),
        out_shape=jax.ShapeDtypeStruct((n_seq, n_pages // PPB, hh, dh), F32),
        compiler_params=_cparams(("parallel", "arbitrary")),
        name="moba_block_means",
    )(page_table.reshape(-1), *([pool_k] * PAGES_PER_STEP))


def _moba_pick_kernel(q_ref, km_ref, o_ref, *, n_blk, t):
    hh, dh = MOBA_HEADS, MOBA_DH
    q = q_ref[0] * (dh ** -0.5)
    cols = []
    for h in range(hh):
        sl = slice(h * dh, (h + 1) * dh)
        cols.append(_dot3(km_ref[0, :, sl], q[:, sl], nt=True))
    gate = jnp.concatenate(cols, axis=1)
    _, cnt = _topk_mask_rows(gate, n_blk, n_blk, MOBA_TOPK)
    jrow = lax.broadcasted_iota(jnp.int32, gate.shape, 0).astype(F32)
    rows = [jnp.sum(jnp.where(cnt == float(r), jrow, 0.0), axis=0, keepdims=True)
            for r in range(MOBA_TOPK)]
    rows.append(jnp.zeros((SUBLANES - MOBA_TOPK, hh * t), F32))
    o_ref[0] = jnp.concatenate(rows, axis=0).astype(jnp.int32)


def moba_pick_blocks(q, kmean):
    n_seq, t, w = q.shape
    n_blk = kmean.shape[1]
    kern = functools.partial(_moba_pick_kernel, n_blk=n_blk, t=t)
    return pl.pallas_call(
        kern,
        grid=(n_seq,),
        in_specs=[pl.BlockSpec((1, t, w), lambda b: (b, 0, 0)),
                  pl.BlockSpec((1, n_blk, w), lambda b: (b, 0, 0))],
        out_specs=pl.BlockSpec((1, SUBLANES, MOBA_HEADS * t), lambda b: (b, 0, 0)),
        out_shape=jax.ShapeDtypeStruct((n_seq, SUBLANES, MOBA_HEADS * t), jnp.int32),
        compiler_params=_cparams(("parallel",)),
        name="moba_pick_blocks",
    )(q, kmean)


def _moba_dec_kernel(pt_ref, sel_ref, q_ref, kn_ref, vn_ref, sl_ref, pk_hbm, pv_hbm, o_ref,
                     kbuf, vbuf, sem, *, layer, t_len, past, n_pages, n_heads, hps, n_steps):
    n_buf = t_len * MOBA_TOPK * PPB
    n_groups = n_heads // hps
    b, hg = pl.program_id(0), pl.program_id(1)
    step = b * n_groups + hg
    slot = step % 2
    page = PAGE_SIZE
    dh = MOBA_DH

    def blk_of(bb, hh, t, r):
        return sel_ref[((bb * MOBA_TOPK + r) * n_heads + hh) * t_len + t]

    def copies(bb, grp, sl):
        out = []
        for hi in range(hps):
            hh = grp * hps + hi
            for t in range(t_len):
                for r in range(MOBA_TOPK):
                    for u in range(PPB):
                        phys = pt_ref[bb * n_pages + blk_of(bb, hh, t, r) * PPB + u]
                        i = hi * n_buf + (t * MOBA_TOPK + r) * PPB + u
                        out.append(pltpu.make_async_copy(pk_hbm.at[layer, phys, :, hh, :], kbuf.at[sl, i], sem.at[0, sl]))
                        out.append(pltpu.make_async_copy(pv_hbm.at[layer, phys, :, hh, :], vbuf.at[sl, i], sem.at[1, sl]))
        return out

    @pl.when(step == 0)
    def _():
        for cp in copies(b, hg, slot):
            cp.start()

    @pl.when(step + 1 < n_steps)
    def _():
        nxt = step + 1
        for cp in copies(nxt // n_groups, nxt % n_groups, 1 - slot):
            cp.start()

    row = lax.broadcasted_iota(jnp.int32, (t_len, page), 0)
    lane = lax.broadcasted_iota(jnp.int32, (t_len, page), 1)
    qi = lax.broadcasted_iota(jnp.int32, (t_len, t_len), 0)
    ki = lax.broadcasted_iota(jnp.int32, (t_len, t_len), 1)

    def tree(xs, op):
        while len(xs) > 1:
            xs = [op(xs[n], xs[n + 1]) for n in range(0, len(xs) - 1, 2)] + ([xs[-1]] if len(xs) % 2 else [])
        return xs[0]

    for i in range(hps * n_buf):
        pltpu.make_async_copy(pk_hbm.at[layer, 0, :, 0, :], kbuf.at[slot, i], sem.at[0, slot]).wait()
        pltpu.make_async_copy(pv_hbm.at[layer, 0, :, 0, :], vbuf.at[slot, i], sem.at[1, slot]).wait()

    his = range(hps)
    cols = [slice(hi * dh, (hi + 1) * dh) for hi in his]
    slopes = [sl_ref[hi][:, :page] for hi in his]
    qs = [(q_ref[0, :, cols[hi]] * (dh ** -0.5)).astype(BF16) for hi in his]
    s_owns = [jnp.where(ki <= qi, _dot_nt(qs[hi], kn_ref[0, :, cols[hi]].astype(BF16))
                        - slopes[hi][:, :t_len] * (qi - ki).astype(F32), NEG) for hi in his]
    scores = [[] for _ in his]
    for t in range(t_len):
        for r in range(MOBA_TOPK):
            for u in range(PPB):
                for hi in his:
                    dist = (past + row) - (blk_of(b, hg * hps + hi, t, r) * MOBA_BLOCK + u * page + lane)
                    s = _dot_nt(qs[hi], kbuf[slot, hi * n_buf + (t * MOBA_TOPK + r) * PPB + u].astype(BF16))
                    scores[hi].append(jnp.where(row == t, s - slopes[hi] * dist.astype(F32), NEG))
    ms = [jnp.maximum(s_owns[hi].max(axis=-1, keepdims=True),
                      tree(scores[hi], jnp.maximum).max(axis=-1, keepdims=True)) for hi in his]
    p_owns = [jnp.exp(s_owns[hi] - ms[hi]) for hi in his]
    ps = [[jnp.exp(s - ms[hi]) for s in scores[hi]] for hi in his]
    inv_ls = [1.0 / (p_owns[hi].sum(axis=-1, keepdims=True) + tree(ps[hi], jnp.add).sum(axis=-1, keepdims=True))
              for hi in his]
    accs = [_dot((p_owns[hi] * inv_ls[hi]).astype(BF16), vn_ref[0, :, cols[hi]].astype(BF16)) for hi in his]
    for i in range(n_buf):
        for hi in his:
            accs[hi] = accs[hi] + _dot((ps[hi][i] * inv_ls[hi]).astype(BF16),
                                       vbuf[slot, hi * n_buf + i].astype(BF16))
    for hi in his:
        o_ref[0, hi] = accs[hi]


def moba_decode_attn(q, k_new, v_new, pool_k, pool_v, layer, page_table, sel):
    n_seq, t_len, w = q.shape
    n_pages = page_table.shape[1]
    hh, dh = MOBA_HEADS, MOBA_DH
    _, _, page, _, _ = pool_k.shape
    past = n_pages * page
    assert past % MOBA_BLOCK == 0
    hps = 2 if hh % 2 == 0 else 1
    n_buf = hps * t_len * MOBA_TOPK * PPB
    row = lambda: pl.BlockSpec((1, t_len, hps * dh), lambda b, g, pt, sl: (b, 0, g))
    kern = functools.partial(_moba_dec_kernel, layer=layer, t_len=t_len, past=past, n_pages=n_pages,
                             n_heads=hh, hps=hps, n_steps=n_seq * (hh // hps))
    return pl.pallas_call(
        kern,
        grid_spec=pltpu.PrefetchScalarGridSpec(
            num_scalar_prefetch=2, grid=(n_seq, hh // hps),
            in_specs=[row(), row(), row(),
                      pl.BlockSpec((hps, 1, MOBA_BLOCK), lambda b, g, pt, sl: (g, 0, 0)),
                      pl.BlockSpec(memory_space=pl.ANY), pl.BlockSpec(memory_space=pl.ANY)],
            out_specs=pl.BlockSpec((1, hps, t_len, dh), lambda b, g, pt, sl: (b, g, 0, 0)),
            scratch_shapes=[pltpu.VMEM((2, n_buf, page, dh), F32), pltpu.VMEM((2, n_buf, page, dh), F32),
                            pltpu.SemaphoreType.DMA((2, 2))]),
        out_shape=jax.ShapeDtypeStruct((n_seq, hh, t_len, dh), F32),
        compiler_params=_cparams(("arbitrary", "arbitrary")),
        name="moba_decode_attn",
    )(page_table.reshape(-1), sel.reshape(-1), q, k_new, v_new, _alibi_rows(hh, MOBA_BLOCK), pool_k, pool_v)


def _last_rows(buf, x, n):
    t = x.shape[1]
    if t >= n:
        return x[:, t - n:]
    return jnp.concatenate([buf[:, buf.shape[1] - (n - t):], x], axis=1)


def _pad_front(buf, rows):
    return jnp.pad(buf, ((0, 0), (rows - buf.shape[1], 0), (0, 0)))


def kernel(x_prompt, x_sample, state_dn_s, state_dn_conv, cache_moba_k, cache_moba_v, page_table, cache_mem_k, cache_mem_v, state_ffn_conv, mem_prompt, norm_mix, norm_xattn, norm_mem, norm_ffn, norm_final, dn_w_in, dn_conv_w, dn_a_log, dn_dt_bias, dn_o_gain, dn_w_out, moba_w_qkv, moba_w_out, xa_w_q, xa_w_kv, xa_w_out, ffn_w_up, ffn_conv_w, ffn_w_down):
    bp, seq, d = x_prompt.shape
    bs, dseq, _ = x_sample.shape
    depth = norm_mix.shape[0]
    mem_len = mem_prompt.shape[1]
    d_ff = ffn_w_down.shape[1]
    qk_w = DN_HEADS * DN_DK
    v_w = DN_HEADS * DN_DV
    conv_ch = 2 * qk_w + v_w
    moba_w = MOBA_HEADS * MOBA_DH
    xa_w = XA_HEADS * XA_DH
    bf = lambda a: a.astype(BF16)

    hp = x_prompt.reshape(bp * seq, d)
    hs = x_sample.reshape(bs * dseq, d)
    outs = {k: [] for k in ("p_dn_s", "p_dn_c", "p_mk", "p_mv", "p_memk", "p_memv", "p_ffc",
                            "s_dn_s", "s_dn_c", "s_mk", "s_mv", "s_ffc")}
    kmeans = {}
    for layer in range(depth):
        if layer % 2 == 0:
            ia = layer // 2
            w_in = dn_w_in[ia]
            ws = [bf(w_in[:, :conv_ch]), bf(w_in[:, conv_ch:conv_ch + v_w]),
                  bf(jnp.pad(w_in[:, conv_ch + v_w:], ((0, 0), (0, LANES - 2 * DN_HEADS))))]
            w_out = bf(dn_w_out[ia])
            args = (dn_conv_w[ia], dn_a_log[ia], dn_dt_bias[ia], dn_o_gain[ia])
            qkv, z, ba = norm_proj(hp, norm_mix[layer], ws, tm=512)
            qkv3 = qkv.reshape(bp, seq, conv_ch)
            tt, hps = 512, 4
            n_steps = bp * (DN_HEADS // hps) * (seq // tt)
            n_pg_all = page_table.shape[0] * page_table.shape[1]
            ride = (layer + 1 < depth and n_pg_all % n_steps == 0 and (n_pg_all // n_steps) % PPB == 0
                    and page_table.shape[1] % (n_pg_all // n_steps) == 0)
            o, s_fin, km = deltanet_core(qkv3, z.reshape(bp, seq, v_w), ba.reshape(bp, seq, LANES),
                                         jnp.zeros((bp, SUBLANES, conv_ch), F32),
                                         jnp.zeros((bp, DN_HEADS, DN_DK, DN_DV), F32), *args,
                                         tt=tt, c=DN_CHUNK, t_valid=seq, hps=hps,
                                         **(dict(pool=cache_moba_k, pool_layer=ia, page_table=page_table)
                                            if ride else {}))
            kmeans[ia] = km
            hp = out_proj_res(o.reshape(bp * seq, v_w), w_out, hp, tm=512)
            outs["p_dn_s"].append(s_fin)
            outs["p_dn_c"].append(_last_rows(jnp.zeros((bp, DN_CONV - 1, conv_ch), F32), qkv3, DN_CONV - 1))
            qkv, z, ba = norm_proj(hs, norm_mix[layer], ws, tm=256)
            qkv3 = qkv.reshape(bs, dseq, conv_ch)
            tpad = 16
            padt = lambda a: jnp.pad(a.reshape(bs, dseq, -1), ((0, 0), (0, tpad - dseq), (0, 0)))
            o, s_fin, _ = deltanet_core(padt(qkv), padt(z), padt(ba),
                                     _pad_front(state_dn_conv[ia], SUBLANES), state_dn_s[ia], *args,
                                     tt=tpad, c=tpad, t_valid=dseq, hps=DN_HEADS)
            hs = out_proj_res(o[:, :dseq].reshape(bs * dseq, v_w), w_out, hs, tm=256)
            outs["s_dn_s"].append(s_fin)
            outs["s_dn_c"].append(_last_rows(state_dn_conv[ia], qkv3, DN_CONV - 1))
        else:
            ib = layer // 2
            w_qkv = moba_w_qkv[ib]
            ws = [bf(w_qkv[:, j * moba_w:(j + 1) * moba_w]) for j in range(3)]
            w_out = bf(moba_w_out[ib])
            q, k, v = (a.reshape(bp, seq, moba_w) for a in norm_proj(hp, norm_mix[layer], ws, tm=512))
            o = moba_prompt_attn(q, k, v)
            hp = out_proj_res(o.reshape(bp * seq, moba_w), w_out, hp, tm=512)
            outs["p_mk"].append(k.reshape(bp, seq, MOBA_HEADS, MOBA_DH))
            outs["p_mv"].append(v.reshape(bp, seq, MOBA_HEADS, MOBA_DH))
            q, k, v = (a.reshape(bs, dseq, moba_w) for a in norm_proj(hs, norm_mix[layer], ws, tm=256))
            kmean = kmeans.get(ib)
            if kmean is None:
                kmean = moba_block_means(cache_moba_k, ib, page_table)
            pick = moba_pick_blocks(q, kmean.reshape(bs, -1, moba_w))
            sel = pick[:, :MOBA_TOPK].reshape(bs, MOBA_TOPK, MOBA_HEADS, dseq)
            o = moba_decode_attn(q, k, v, cache_moba_k, cache_moba_v, ib, page_table, sel)
            o = o.transpose(0, 2, 1, 3).reshape(bs * dseq, moba_w)
            hs = out_proj_res(o, w_out, hs, tm=256)
            outs["s_mk"].append(k.reshape(bs, dseq, MOBA_HEADS, MOBA_DH))
            outs["s_mv"].append(v.reshape(bs, dseq, MOBA_HEADS, MOBA_DH))

        w_kv = xa_w_kv[layer]
        mk, mv = norm_proj(mem_prompt.reshape(bp * mem_len, d), norm_mem[layer],
                           [bf(w_kv[:, :xa_w]), bf(w_kv[:, xa_w:])], tm=256)
        mk = mk.reshape(bp, mem_len, xa_w)
        mv = mv.reshape(bp, mem_len, xa_w)
        outs["p_memk"].append(mk.reshape(bp, mem_len, XA_HEADS, XA_DH))
        outs["p_memv"].append(mv.reshape(bp, mem_len, XA_HEADS, XA_DH))
        w_q, w_o = bf(xa_w_q[layer]), bf(xa_w_out[layer])
        hp = mem_xattn(hp, mk, mv, norm_xattn[layer], w_q, w_o, nb=1, tm=512)
        hs = mem_xattn(hs, cache_mem_k[layer].reshape(bs, mem_len, xa_w),
                       cache_mem_v[layer].reshape(bs, mem_len, xa_w),
                       norm_xattn[layer], w_q, w_o, nb=8, tm=dseq)

        last = layer == depth - 1
        w_up, w_dn = bf(ffn_w_up[layer]), bf(ffn_w_down[layer])
        hp3, tail = conv_ffn(hp.reshape(bp, seq, d), jnp.zeros((bp, SUBLANES, 2 * d_ff), F32),
                             norm_ffn[layer], w_up, ffn_conv_w[layer], w_dn, norm_final,
                             tm=512, stride=1, final_norm=last)
        hp = hp3.reshape(bp * seq, d)
        outs["p_ffc"].append(tail[:, SUBLANES - (FFN_CONV - 1):])
        hs_tm = hs.reshape(bs, dseq, d).transpose(1, 0, 2).reshape(1, dseq * bs, d)
        buf_tm = state_ffn_conv[layer].transpose(1, 0, 2).reshape(1, (FFN_CONV - 1) * bs, 2 * d_ff)
        hs3, tail = conv_ffn(hs_tm, buf_tm, norm_ffn[layer], w_up, ffn_conv_w[layer], w_dn, norm_final,
                             tm=dseq * bs, stride=bs, final_norm=last)
        hs = hs3.reshape(dseq, bs, d).transpose(1, 0, 2).reshape(bs * dseq, d)
        outs["s_ffc"].append(tail.reshape(FFN_CONV - 1, bs, 2 * d_ff).transpose(1, 0, 2))

    y_prompt = hp.reshape(bp, seq, d)
    y_sample = hs.reshape(bs, dseq, d)
    st = lambda k: outs[k][0][None] if len(outs[k]) == 1 else jnp.stack(outs[k])
    return (y_prompt, y_sample, st("p_dn_s"), st("p_dn_c"), st("p_mk"), st("p_mv"),
            st("p_memk"), st("p_memv"), st("p_ffc"),
            st("s_dn_s"), st("s_dn_c"), st("s_mk"), st("s_mv"), st("s_ffc"))
```

```python
import functools

import jax
import jax.numpy as jnp
from jax import lax
from jax.experimental import pallas as pl
from jax.experimental.pallas import tpu as pltpu

F32 = jnp.float32
BF16 = jnp.bfloat16
EPS = 1e-6
NEG = -1e30

DN_HEADS = 8
DN_DK = 128
DN_DV = 128
DN_CONV = 4
DN_CHUNK = 64
MOBA_HEADS = 8
MOBA_DH = 128
MOBA_BLOCK = 256
MOBA_TOPK = 3
PAGE_SIZE = 128
XA_HEADS = 4
XA_DH = 128
FFN_CONV = 3

SUBLANES = 8
LANES = 128
VMEM_LIMIT = 56 * 1024 * 1024


def _cparams(sem):
    return pltpu.CompilerParams(dimension_semantics=sem, vmem_limit_bytes=VMEM_LIMIT)


def _resident(shape):
    nd = len(shape)
    return pl.BlockSpec(shape, lambda *_: (0,) * nd, pipeline_mode=pl.Buffered(1))


def _rms(x, gain):
    return x * lax.rsqrt(jnp.mean(x * x, axis=-1, keepdims=True) + EPS) * gain


def _silu(x):
    return x * jax.nn.sigmoid(x)


def _dot(a, b):
    return jnp.dot(a, b, preferred_element_type=F32)


def _dot_nt(a, b):
    return lax.dot_general(a, b, (((1,), (1,)), ((), ())), preferred_element_type=F32)


def _split2(a):
    hi = a.astype(BF16)
    lo = (a - hi.astype(F32)).astype(BF16)
    return hi, lo


def _dot3(a, b, nt=False):
    d = _dot_nt if nt else _dot
    ah, al = _split2(a)
    bh, bl = _split2(b)
    return d(ah, bh) + (d(al, bh) + d(ah, bl))


def _split3(a):
    p1 = a.astype(BF16)
    r1 = a - p1.astype(F32)
    p2 = r1.astype(BF16)
    r2 = r1 - p2.astype(F32)
    return p1, p2, r2.astype(BF16)


def _dot_sel(sel, x, nt=False):
    d = _dot_nt if nt else _dot
    p1, p2, p3 = _split3(x)
    return d(sel, p1) + (d(sel, p2) + d(sel, p3))


def _norm_proj_kernel(x_ref, g_ref, *refs, n_w):
    w_refs, o_refs = refs[:n_w], refs[n_w:]
    xn = _rms(x_ref[...], g_ref[...]).astype(BF16)
    for w_ref, o_ref in zip(w_refs, o_refs):
        n = w_ref.shape[1]
        for c in range(0, n, 512):
            cw = min(512, n - c)
            o_ref[:, c:c + cw] = _dot(xn, w_ref[:, c:c + cw])


def norm_proj(x, gain, ws, tm):
    r, d = x.shape
    tm = min(tm, r)
    assert r % tm == 0
    n_w = len(ws)
    return pl.pallas_call(
        functools.partial(_norm_proj_kernel, n_w=n_w),
        grid=(r // tm,),
        in_specs=[pl.BlockSpec((tm, d), lambda i: (i, 0)), _resident((1, d))]
        + [_resident(w.shape) for w in ws],
        out_specs=[pl.BlockSpec((tm, w.shape[1]), lambda i: (i, 0)) for w in ws],
        out_shape=[jax.ShapeDtypeStruct((r, w.shape[1]), F32) for w in ws],
        compiler_params=_cparams(("parallel",)),
        name="norm_proj",
    )(x, gain.reshape(1, d), *ws)


def _shift_rows(u, head, stride):
    rows = u.shape[0]
    if stride % SUBLANES == 0:
        return jnp.concatenate([head, u[:rows - stride]], axis=0)
    assert stride == 1
    rolled = pltpu.roll(u, 1, axis=0)
    first = jnp.where(lax.broadcasted_iota(jnp.int32, (SUBLANES, u.shape[1]), 0) == 0, head, rolled[:SUBLANES])
    return jnp.concatenate([first, rolled[SUBLANES:]], axis=0)


def _ffn_kernel(x_ref, buf_ref, g_ref, wup_ref, cw_ref, wdn_ref, fg_ref, o_ref, tail_ref, carry, hid,
                *, tm, stride, pad, d_ff, final_norm, chunk):
    t = pl.program_id(1)

    @pl.when(t == 0)
    def _():
        carry[...] = buf_ref[0]

    x = x_ref[0]
    xn = _rms(x, g_ref[...]).astype(BF16)
    for j0 in range(0, d_ff, chunk):
        cs = []
        for c0 in (j0, d_ff + j0):
            cols = slice(c0, c0 + chunk)
            u = _dot(xn, wup_ref[:, cols])
            prev = carry[:, cols]
            u1 = _shift_rows(u, prev[pad - stride:], stride)
            u2 = _shift_rows(u1, prev[pad - 2 * stride:pad - stride], stride)
            cs.append(u2 * cw_ref[0:1, cols] + u1 * cw_ref[1:2, cols] + u * cw_ref[2:3, cols])
            carry[:, cols] = u[tm - pad:]
        hid[:, j0:j0 + chunk] = (_silu(cs[0]) * cs[1]).astype(BF16)
    out = x + _dot(hid[...], wdn_ref[...])
    if final_norm:
        out = _rms(out, fg_ref[...])
    o_ref[0] = out
    tail_ref[0] = carry[...]


def conv_ffn(h, buf, gain, w_up, conv_w, w_down, final_gain, *, tm, stride, final_norm):
    nb, t, d = h.shape
    pad = buf.shape[1]
    d_ff = w_down.shape[0]
    tm = min(tm, t)
    assert t % tm == 0 and pad % SUBLANES == 0 and pad >= 2 * stride and tm >= pad
    kern = functools.partial(_ffn_kernel, tm=tm, stride=stride, pad=pad, d_ff=d_ff,
                             final_norm=final_norm, chunk=256)
    return pl.pallas_call(
        kern,
        grid=(nb, t // tm),
        in_specs=[pl.BlockSpec((1, tm, d), lambda b, i: (b, i, 0)),
                  pl.BlockSpec((1, pad, 2 * d_ff), lambda b, i: (b, 0, 0)),
                  _resident((1, d)), _resident(w_up.shape), _resident(conv_w.shape),
                  _resident(w_down.shape), _resident((1, d))],
        out_specs=[pl.BlockSpec((1, tm, d), lambda b, i: (b, i, 0)),
                   pl.BlockSpec((1, pad, 2 * d_ff), lambda b, i: (b, 0, 0))],
        out_shape=[jax.ShapeDtypeStruct((nb, t, d), F32),
                   jax.ShapeDtypeStruct((nb, pad, 2 * d_ff), F32)],
        scratch_shapes=[pltpu.VMEM((pad, 2 * d_ff), F32), pltpu.VMEM((tm, d_ff), BF16)],
        compiler_params=_cparams(("parallel", "arbitrary")),
        name="conv_ffn",
    )(h, buf, gain.reshape(1, d), w_up, conv_w, w_down, final_gain.reshape(1, d))


def _xattn_kernel(x_ref, a_ref, wa_ref, mk_ref, mv_ref, g_ref, wq_ref, wo_ref, o_ref, att, *, nb, tm):
    x = x_ref[...] + _dot(a_ref[...].astype(BF16), wa_ref[...])
    xn = _rms(x, g_ref[...]).astype(BF16)
    q = _dot(xn, wq_ref[...])
    scale = XA_DH ** -0.5
    for b in range(nb):
        rows = slice(b * tm, (b + 1) * tm)
        for hh in range(XA_HEADS):
            cols = slice(hh * XA_DH, (hh + 1) * XA_DH)
            s = _dot_nt(q[rows, cols].astype(BF16), mk_ref[b, :, cols].astype(BF16)) * scale
            p = jnp.exp(s - jnp.max(s, axis=-1, keepdims=True))
            p = p / jnp.sum(p, axis=-1, keepdims=True)
            att[rows, cols] = _dot(p.astype(BF16), mv_ref[b, :, cols].astype(BF16))
    o_ref[...] = x + _dot(att[...].astype(BF16), wo_ref[...])


def mem_xattn(h, a, w_a, mem_k, mem_v, gain, w_q, w_o, *, nb, tm):
    r, d = h.shape
    ka = a.shape[1]
    n_seq, m, w = mem_k.shape
    t = r // n_seq
    assert t % tm == 0 and (nb == 1 or tm == t) and n_seq % nb == 0
    tiles = t // tm
    kern = functools.partial(_xattn_kernel, nb=nb, tm=tm)
    return pl.pallas_call(
        kern,
        grid=(n_seq // nb, tiles),
        in_specs=[pl.BlockSpec((nb * tm, d), lambda b, i: (b * tiles + i, 0)),
                  pl.BlockSpec((nb * tm, ka), lambda b, i: (b * tiles + i, 0)),
                  _resident(w_a.shape),
                  pl.BlockSpec((nb, m, w), lambda b, i: (b, 0, 0)),
                  pl.BlockSpec((nb, m, w), lambda b, i: (b, 0, 0)),
                  _resident((1, d)), _resident(w_q.shape), _resident(w_o.shape)],
        out_specs=pl.BlockSpec((nb * tm, d), lambda b, i: (b * tiles + i, 0)),
        out_shape=jax.ShapeDtypeStruct((r, d), F32),
        scratch_shapes=[pltpu.VMEM((nb * tm, w), F32)],
        compiler_params=_cparams(("parallel", "arbitrary")),
        name="mem_xattn",
    )(h, a, w_a, mem_k, mem_v, gain.reshape(1, d), w_q, w_o)


def _tri_inv(a, c):
    r = lax.broadcasted_iota(jnp.int32, (c, c), 0)
    q = lax.broadcasted_iota(jnp.int32, (c, c), 1)
    eye = (r == q).astype(F32)
    x = [eye - jnp.where((r >> 1) == (q >> 1), ai, 0.0) for ai in a]
    sh = 1
    while (1 << sh) < c:
        sh += 1
        mask = ((r >> sh) == (q >> sh)) & ((r >> (sh - 1)) != (q >> (sh - 1)))
        xb = [xi.astype(BF16) for xi in x]
        lx = [_dot(jnp.where(mask, ai, 0.0).astype(BF16), xbi) for ai, xbi in zip(a, xb)]
        x = [xi - _dot(xbi, li.astype(BF16)) for xi, xbi, li in zip(x, xb, lx)]
    return x


def _page_means_side_job(pt_ref, pool_hbm, km_ref, pbuf, psem, *, step, n_steps, layer, pps):
    slot = step % 2

    def copies(st, sl):
        return [pltpu.make_async_copy(pool_hbm.at[layer, pt_ref[st * pps + u]], pbuf.at[sl, u], psem.at[sl])
                for u in range(pps)]

    @pl.when(step == 0)
    def _():
        for cp in copies(step, slot):
            cp.start()

    @pl.when(step + 1 < n_steps)
    def _():
        for cp in copies(step + 1, 1 - slot):
            cp.start()

    for cp in copies(step, slot):
        cp.wait()
    for blk in range(pps // PPB):
        tot = jnp.sum(pbuf[slot, blk * PPB], axis=0)
        for w in range(1, PPB):
            tot = tot + jnp.sum(pbuf[slot, blk * PPB + w], axis=0)
        km_ref[0, blk] = tot * (1.0 / MOBA_BLOCK)


def _dn_kernel(*refs, tt, c, t_valid, nt, hps, pages):
    if pages is not None:
        pt_ref, refs = refs[0], refs[1:]
    (q_ref, k_ref, v_ref, z_ref, ba_ref, bq_ref, bk_ref, bv_ref, s0_ref,
     cwq_ref, cwk_ref, cwv_ref, avec_ref, dtb_ref, og_ref) = refs[:15]
    refs = refs[15:]
    if pages is not None:
        pool_hbm, o_ref, sfin_ref, km_ref, xq, xk, xv, state, pbuf, psem = refs
    else:
        o_ref, sfin_ref, xq, xk, xv, state = refs
    hg = pl.program_id(1)
    t = pl.program_id(2)
    lead = SUBLANES
    if pages is not None:
        step = (pl.program_id(0) * pl.num_programs(1) + hg) * nt + t
        _page_means_side_job(pt_ref, pool_hbm, km_ref, pbuf, psem, step=step, **pages)

    @pl.when(t == 0)
    def _():
        state[...] = s0_ref[0]
        xq[...] = bq_ref[0]
        xk[...] = bk_ref[0]
        xv[...] = bv_ref[0]

    def conv_silu(hist, x_ref, cw_ref):
        x = x_ref[0]
        taps = [x]
        for j in range(1, DN_CONV):
            taps.append(_shift_rows(taps[-1], hist[lead - j:lead - j + 1], 1))
        y = taps[DN_CONV - 1] * cw_ref[0:1]
        for j in range(1, DN_CONV):
            y = y + taps[DN_CONV - 1 - j] * cw_ref[j:j + 1]
        hist[...] = x[tt - lead:]
        return _silu(y)

    qa = conv_silu(xq, q_ref, cwq_ref)
    ka = conv_silu(xk, k_ref, cwk_ref)
    va = conv_silu(xv, v_ref, cwv_ref)

    ba = ba_ref[0]
    lane = lax.broadcasted_iota(jnp.int32, (tt, LANES), 1)
    beta_all = jax.nn.sigmoid(ba)
    g_all = -avec_ref[...] * jax.nn.softplus(ba + dtb_ref[...])
    if t_valid < nt * tt:
        row = t * tt + lax.broadcasted_iota(jnp.int32, (tt, LANES), 0)
        beta_all = jnp.where(row < t_valid, beta_all, 0.0)
        g_all = jnp.where(row < t_valid, g_all, 0.0)

    ri = lax.broadcasted_iota(jnp.int32, (c, c), 0)
    ci = lax.broadcasted_iota(jnp.int32, (c, c), 1)
    causal = ri >= ci
    strict = ri > ci
    ltri = causal.astype(BF16)
    ones8 = jnp.ones((SUBLANES, LANES), BF16)
    og = og_ref[...]

    q, k, v, beta, g_sel = [], [], [], [], []
    for hh in range(hps):
        cols = slice(hh * DN_DK, (hh + 1) * DN_DK)
        head = hg * hps + hh
        qh, kh = qa[:, cols], ka[:, cols]
        q.append(qh * lax.rsqrt(jnp.sum(qh * qh, axis=-1, keepdims=True) + EPS) * (DN_DK ** -0.5))
        k.append(kh * lax.rsqrt(jnp.sum(kh * kh, axis=-1, keepdims=True) + EPS))
        v.append(va[:, hh * DN_DV:(hh + 1) * DN_DV])
        beta.append(jnp.sum(jnp.where(lane == head, beta_all, 0.0), axis=1, keepdims=True))
        g_sel.append(jnp.where(lane == head + DN_HEADS, g_all, 0.0))

    n_ch = tt // c
    items = [(hh, slice(ch * c, (ch + 1) * c)) for ch in range(n_ch) for hh in range(hps)]
    gcs = [_dot_sel(ltri, g_sel[hh][sl]) for hh, sl in items]
    gcols = [jnp.sum(gc, axis=1, keepdims=True) for gc in gcs]
    grows = [_dot_sel(ones8, gc, nt=True)[0:1] for gc in gcs]
    egs = [jnp.exp(gcol) for gcol in gcols]
    kbs = [k[hh][sl] * beta[hh][sl] for hh, sl in items]
    kqs = [_dot_nt(jnp.concatenate([kb, q[hh][sl]], axis=0).astype(BF16), k[hh][sl].astype(BF16))
           for kb, (hh, sl) in zip(kbs, items)]
    gams = [jnp.exp(jnp.where(causal, gcol - grow, NEG)) for gcol, grow in zip(gcols, grows)]
    t_invs = _tri_inv([jnp.where(strict, kq[:c] * gam, 0.0) for kq, gam in zip(kqs, gams)], c)
    uws = [_dot(ti.astype(BF16),
                jnp.concatenate([v[hh][sl] * beta[hh][sl], kb * eg], axis=1).astype(BF16)).astype(BF16)
           for ti, (hh, sl), kb, eg in zip(t_invs, items, kbs, egs)]
    qkuws = [_dot((kq[c:] * gam).astype(BF16), uw) for kq, gam, uw in zip(kqs, gams, uws)]
    glasts = [gcol[c - 1:c] for gcol in gcols]
    kduws = [lax.dot_general((k[hh][sl] * jnp.exp(glast - gcol)).astype(BF16), uw, (((0,), (0,)), ((), ())),
                             preferred_element_type=F32)
             for (hh, sl), glast, gcol, uw in zip(items, glasts, gcols, uws)]
    lhss = [jnp.concatenate([q[hh][sl] * eg - qkuw[:, DN_DV:], kduw[:, DN_DV:]], axis=0).astype(BF16)
            for (hh, sl), eg, qkuw, kduw in zip(items, egs, qkuws, kduws)]

    s = [state[hh] for hh in range(hps)]
    for (hh, sl), lhs, qkuw, kduw, glast in zip(items, lhss, qkuws, kduws, glasts):
        r = _dot(lhs, s[hh].astype(BF16))
        o = r[:c] + qkuw[:, :DN_DV]
        s[hh] = s[hh] * jnp.exp(glast) - r[c:] + kduw[:, :DN_DV]
        o = o * lax.rsqrt(jnp.mean(o * o, axis=-1, keepdims=True) + EPS) * og
        cols = slice(hh * DN_DV, (hh + 1) * DN_DV)
        o_ref[0, sl, cols] = o * _silu(z_ref[0, sl, cols])
    for hh in range(hps):
        state[hh] = s[hh]

    @pl.when(t == nt - 1)
    def _():
        sfin_ref[0] = state[...]


def deltanet_core(qkv, z, ba, buf, s0, conv_w, a_log, dt_bias, o_gain, *, tt, c, t_valid, hps,
                  pool=None, pool_layer=0, page_table=None):
    b, tp, _ = qkv.shape
    hh = DN_HEADS
    nt = tp // tt
    assert tp % tt == 0 and tt % c == 0 and hh % hps == 0
    ng = hh // hps
    zpad = jnp.zeros((LANES - 2 * hh,), F32)
    avec = jnp.concatenate([jnp.zeros((hh,), F32), jnp.exp(a_log), zpad]).reshape(1, LANES)
    dtb = jnp.concatenate([jnp.zeros((hh,), F32), dt_bias, zpad]).reshape(1, LANES)
    wq, wv = hps * DN_DK, hps * DN_DV
    col = lambda part: pl.BlockSpec((1, tt, wq), lambda i, g, t, *_: (i, t, part * ng + g))
    bufc = lambda part: pl.BlockSpec((1, SUBLANES, wq), lambda i, g, t, *_: (i, 0, part * ng + g))
    cwc = lambda part: pl.BlockSpec((DN_CONV, wq), lambda i, g, t, *_: (0, part * ng + g))
    in_specs = [col(0), col(1), col(2),
                pl.BlockSpec((1, tt, wv), lambda i, g, t, *_: (i, t, g)),
                pl.BlockSpec((1, tt, LANES), lambda i, g, t, *_: (i, t, 0)),
                bufc(0), bufc(1), bufc(2),
                pl.BlockSpec((1, hps, DN_DK, DN_DV), lambda i, g, t, *_: (i, g, 0, 0)),
                cwc(0), cwc(1), cwc(2),
                pl.BlockSpec((1, LANES), lambda i, g, t, *_: (0, 0)),
                pl.BlockSpec((1, LANES), lambda i, g, t, *_: (0, 0)),
                pl.BlockSpec((1, DN_DV), lambda i, g, t, *_: (0, 0))]
    out_specs = [pl.BlockSpec((1, tt, wv), lambda i, g, t, *_: (i, t, g)),
                 pl.BlockSpec((1, hps, DN_DK, DN_DV), lambda i, g, t, *_: (i, g, 0, 0))]
    out_shape = [jax.ShapeDtypeStruct((b, tp, hh * DN_DV), F32),
                 jax.ShapeDtypeStruct((b, hh, DN_DK, DN_DV), F32)]
    scratch = [pltpu.VMEM((SUBLANES, wq), F32), pltpu.VMEM((SUBLANES, wq), F32),
               pltpu.VMEM((SUBLANES, wv), F32), pltpu.VMEM((hps, DN_DK, DN_DV), F32)]
    args = [qkv, qkv, qkv, z, ba, buf, buf, buf, s0, conv_w, conv_w, conv_w, avec, dtb,
            o_gain.reshape(1, DN_DV)]
    pages = None
    prefetch = []
    sem = ("parallel", "parallel", "arbitrary")
    if pool is not None:
        n_steps = b * ng * nt
        n_seq, n_pages = page_table.shape
        _, _, page, ph, pdh = pool.shape
        assert (n_seq * n_pages) % n_steps == 0
        pps = (n_seq * n_pages) // n_steps
        assert pps % PPB == 0 and n_pages % pps == 0
        pages = dict(n_steps=n_steps, layer=pool_layer, pps=pps)
        prefetch = [page_table.reshape(-1)]
        in_specs.append(pl.BlockSpec(memory_space=pl.ANY))
        args.append(pool)
        out_specs.append(pl.BlockSpec((1, pps // PPB, ph, pdh),
                                      lambda i, g, t, *_: ((i * ng + g) * nt + t, 0, 0, 0)))
        out_shape.append(jax.ShapeDtypeStruct((n_steps, pps // PPB, ph, pdh), F32))
        scratch += [pltpu.VMEM((2, pps, page, ph, pdh), F32), pltpu.SemaphoreType.DMA((2,))]
        sem = ("arbitrary", "arbitrary", "arbitrary")
    kern = functools.partial(_dn_kernel, tt=tt, c=c, t_valid=t_valid, nt=nt, hps=hps, pages=pages)
    res = pl.pallas_call(
        kern,
        grid_spec=pltpu.PrefetchScalarGridSpec(
            num_scalar_prefetch=len(prefetch), grid=(b, ng, nt),
            in_specs=in_specs, out_specs=out_specs, scratch_shapes=scratch),
        out_shape=out_shape,
        compiler_params=_cparams(sem),
        name="deltanet_core",
    )(*prefetch, *args)
    if pool is None:
        return res[0], res[1], None
    return res[0], res[1], res[2].reshape(n_seq, n_pages // PPB, ph, pdh)


def _topk_mask_rows(g, n_valid, n_rows, k_top):
    jrow = lax.broadcasted_iota(jnp.int32, g.shape, 0)
    cnt = jnp.zeros(g.shape, F32)
    for jp in range(min(n_rows, n_valid)):
        gb = g[jp:jp + 1, :]
        cnt = cnt + jnp.where(gb > g, 1.0, jnp.where(gb == g, jnp.where(jrow > jp, 1.0, 0.0), 0.0))
    return jnp.where(jrow < n_valid, cnt, float(k_top)) < float(k_top), cnt


LOG2E = 1.4426950408889634
N_EXT = 16


def _split3_f32(a):
    p1 = a.astype(BF16).astype(F32)
    r1 = a - p1
    p2 = r1.astype(BF16).astype(F32)
    return p1, p2, (r1 - p2).astype(BF16).astype(F32)


def _moba_kernel(q_ref, k_ref, v_ref, sl_ref, o_ref, kmean, kaug, vt, qtb_s, pen, sbuf, *, nb, nbp, grp, qb):
    i = pl.program_id(2)
    blk, dh = MOBA_BLOCK, MOBA_DH
    slope = sl_ref[0]

    @pl.when(i == 0)
    def _():
        kmean[...] = jnp.zeros_like(kmean)
        lane = lax.broadcasted_iota(jnp.int32, (blk, LANES), 1)
        crow = lax.broadcasted_iota(jnp.int32, (blk, LANES), 0).astype(F32)
        kext = jnp.where(lane < 3, crow, jnp.where(lane < 6, 1.0, 0.0)).astype(BF16)
        for j in range(nb):
            rows = slice(j * blk, (j + 1) * blk)
            kj = k_ref[0, rows, :]
            kmean[j:j + 1, :] = jnp.sum(kj, axis=0, keepdims=True) * (1.0 / blk)
            kaug[j] = jnp.concatenate([kj.astype(BF16), kext], axis=1)
            vt[j] = v_ref[0, rows, :].T.astype(BF16)
        km = kmean[...]
        jrow = lax.broadcasted_iota(jnp.int32, (nbp, blk), 0)
        ridx = lax.broadcasted_iota(jnp.int32, (nbp, blk), 1)
        for jq in range(nb):
            qt = (q_ref[0, jq * blk:(jq + 1) * blk, :] * (dh ** -0.5)).T
            qtb_s[jq] = (qt * LOG2E).astype(BF16)
            sel, _ = _topk_mask_rows(_dot3(km, qt), jq, nb, MOBA_TOPK)
            add = jnp.where(jrow == jq, 0.0, jnp.where(sel, 0.0, NEG))
            row_j = LOG2E * (add - slope * ((jq - jrow) * blk + ridx).astype(F32))
            for n, piece in enumerate(_split3_f32(row_j)):
                pen[n, jq] = piece

    qw = qb * blk
    iqs = [i * qb + n for n in range(qb)]
    cat = lambda xs: xs[0] if len(xs) == 1 else jnp.concatenate(xs, axis=1)
    qtb = cat([qtb_s[iq] for iq in iqs])
    slope_pieces = _split3_f32(jnp.concatenate([slope] * qb, axis=1) * LOG2E) if qb > 1 \
        else _split3_f32(slope * LOG2E)
    row16 = lax.broadcasted_iota(jnp.int32, (N_EXT, qw), 0)
    zpad = jnp.zeros((LANES - N_EXT, qw), BF16)

    def q_aug(j):
        pieces = [cat([pen[n, iq, pl.ds(j, 1), :] for iq in iqs]) for n in range(3)]
        ext = jnp.zeros((N_EXT, qw), F32)
        for n, piece in enumerate(tuple(slope_pieces) + tuple(pieces)):
            ext = jnp.where(row16 == n, piece, ext)
        return jnp.concatenate([qtb, ext.astype(BF16), zpad], axis=0)

    key = lax.broadcasted_iota(jnp.int32, (blk, qw), 0)
    col = lax.broadcasted_iota(jnp.int32, (blk, qw), 1)
    key_minus_qry = key - (col & (blk - 1))
    col_blk = lax.broadcasted_iota(jnp.int32, (1, qw), 1) // blk

    def fold8(x, op):
        return op(x.reshape(blk // SUBLANES, SUBLANES, qw), axis=0)

    n_trips = iqs[-1] // grp + 1

    def pass_scores(g, m8):
        for u in range(grp):
            j = g * grp + u
            s = _dot(kaug[j], q_aug(j))
            thr = jnp.full((1, qw), blk, jnp.int32)
            for n, iq in enumerate(iqs):
                thr = jnp.where((col_blk == n) & (j == iq), 0, thr)
            s = jnp.where(key_minus_qry > thr, NEG, s)
            sbuf[j] = s
            m8 = jnp.maximum(m8, fold8(s, jnp.max))
        return m8

    m8 = lax.fori_loop(0, n_trips, pass_scores, jnp.full((SUBLANES, qw), NEG, F32))
    m = jnp.max(m8, axis=0, keepdims=True)

    def pass_values(g, carry):
        l8, acc = carry
        for u in range(grp):
            j = g * grp + u
            p = jnp.exp2(sbuf[j] - m)
            l8 = l8 + fold8(p, jnp.sum)
            acc = acc + _dot(vt[j], p.astype(BF16))
        return l8, acc

    l8, acc = lax.fori_loop(0, n_trips, pass_values,
                            (jnp.zeros((SUBLANES, qw), F32), jnp.zeros((dh, qw), F32)))
    o_ref[0] = (acc / jnp.sum(l8, axis=0, keepdims=True)).T


def _alibi_rows(n_heads, width):
    slopes = jnp.exp2(-8.0 * jnp.arange(1, n_heads + 1, dtype=F32) / n_heads)
    return jnp.broadcast_to(slopes[:, None, None], (n_heads, 1, width))


def moba_prompt_attn(q, k, v):
    b, t, w = q.shape
    hh, dh, blk = MOBA_HEADS, MOBA_DH, MOBA_BLOCK
    assert t % blk == 0
    nb = t // blk
    nbp = -(-nb // SUBLANES) * SUBLANES
    grp = next(g for g in (4, 2, 1) if nb % g == 0)
    qb = next(g for g in (4, 2, 1) if nb % g == 0)
    kern = functools.partial(_moba_kernel, nb=nb, nbp=nbp, grp=grp, qb=qb)
    return pl.pallas_call(
        kern,
        grid=(b, hh, nb // qb),
        in_specs=[pl.BlockSpec((1, t, dh), lambda bi, h, i: (bi, 0, h)),
                  pl.BlockSpec((1, t, dh), lambda bi, h, i: (bi, 0, h)),
                  pl.BlockSpec((1, t, dh), lambda bi, h, i: (bi, 0, h)),
                  pl.BlockSpec((1, 1, blk), lambda bi, h, i: (h, 0, 0))],
        out_specs=pl.BlockSpec((1, qb * blk, dh), lambda bi, h, i: (bi, i, h)),
        out_shape=jax.ShapeDtypeStruct((b, t, w), F32),
        scratch_shapes=[pltpu.VMEM((nbp, dh), F32), pltpu.VMEM((nb, blk, dh + LANES), BF16),
                        pltpu.VMEM((nb, dh, blk), BF16), pltpu.VMEM((nb, dh, blk), BF16),
                        pltpu.VMEM((3, nb, nbp, blk), F32), pltpu.VMEM((nb, blk, qb * blk), F32)],
        compiler_params=_cparams(("parallel", "parallel", "arbitrary")),
        name="moba_prompt_attn",
    )(q, k, v, _alibi_rows(hh, blk))


PAGES_PER_STEP = 16
PPB = MOBA_BLOCK // PAGE_SIZE


def _kmean_kernel(pt_ref, *refs):
    page_refs, o_ref = refs[:PAGES_PER_STEP], refs[PAGES_PER_STEP]
    for u in range(0, PAGES_PER_STEP, PPB):
        tot = jnp.sum(page_refs[u][...], axis=0)
        for w in range(1, PPB):
            tot = tot + jnp.sum(page_refs[u + w][...], axis=0)
        o_ref[0, u // PPB] = tot * (1.0 / MOBA_BLOCK)


def moba_block_means(pool_k, layer, page_table):
    n_seq, n_pages = page_table.shape
    _, _, page, hh, dh = pool_k.shape
    assert n_pages % PAGES_PER_STEP == 0 and PAGES_PER_STEP % PPB == 0
    steps = n_pages // PAGES_PER_STEP

    def page_spec(u):
        return pl.BlockSpec((None, None, page, hh, dh),
                            lambda b, g, pt: (layer, pt[b * n_pages + g * PAGES_PER_STEP + u], 0, 0, 0))

    bps = PAGES_PER_STEP // PPB
    return pl.pallas_call(
        _kmean_kernel,
        grid_spec=pltpu.PrefetchScalarGridSpec(
            num_scalar_prefetch=1, grid=(n_seq, steps),
            in_specs=[page_spec(u) for u in range(PAGES_PER_STEP)],
            out_specs=pl.BlockSpec((1, bps, hh, dh), lambda b, g, pt: (b, g, 0, 0))),
        out_shape=jax.ShapeDtypeStruct((n_seq, n_pages // PPB, hh, dh), F32),
        compiler_params=_cparams(("parallel", "arbitrary")),
        name="moba_block_means",
    )(page_table.reshape(-1), *([pool_k] * PAGES_PER_STEP))


def _moba_pick_kernel(q_ref, km_ref, o_ref, *, n_blk, t):
    hh, dh = MOBA_HEADS, MOBA_DH
    q = q_ref[0] * (dh ** -0.5)
    cols = []
    for h in range(hh):
        sl = slice(h * dh, (h + 1) * dh)
        cols.append(_dot3(km_ref[0, :, sl], q[:, sl], nt=True))
    gate = jnp.concatenate(cols, axis=1)
    _, cnt = _topk_mask_rows(gate, n_blk, n_blk, MOBA_TOPK)
    jrow = lax.broadcasted_iota(jnp.int32, gate.shape, 0).astype(F32)
    rows = [jnp.sum(jnp.where(cnt == float(r), jrow, 0.0), axis=0, keepdims=True)
            for r in range(MOBA_TOPK)]
    rows.append(jnp.zeros((SUBLANES - MOBA_TOPK, hh * t), F32))
    o_ref[0] = jnp.concatenate(rows, axis=0).astype(jnp.int32)


def moba_pick_blocks(q, kmean):
    n_seq, t, w = q.shape
    n_blk = kmean.shape[1]
    kern = functools.partial(_moba_pick_kernel, n_blk=n_blk, t=t)
    return pl.pallas_call(
        kern,
        grid=(n_seq,),
        in_specs=[pl.BlockSpec((1, t, w), lambda b: (b, 0, 0)),
                  pl.BlockSpec((1, n_blk, w), lambda b: (b, 0, 0))],
        out_specs=pl.BlockSpec((1, SUBLANES, MOBA_HEADS * t), lambda b: (b, 0, 0)),
        out_shape=jax.ShapeDtypeStruct((n_seq, SUBLANES, MOBA_HEADS * t), jnp.int32),
        compiler_params=_cparams(("parallel",)),
        name="moba_pick_blocks",
    )(q, kmean)


def _moba_dec_kernel(pt_ref, sel_ref, q_ref, kn_ref, vn_ref, sl_ref, pk_hbm, pv_hbm, o_ref,
                     kbuf, vbuf, sem, *, layer, t_len, past, n_pages, n_heads, hps, n_steps):
    n_buf = t_len * MOBA_TOPK * PPB
    n_groups = n_heads // hps
    b, hg = pl.program_id(0), pl.program_id(1)
    step = b * n_groups + hg
    slot = step % 2
    page = PAGE_SIZE
    dh = MOBA_DH

    def blk_of(bb, hh, t, r):
        return sel_ref[((bb * MOBA_TOPK + r) * n_heads + hh) * t_len + t]

    def copies(bb, grp, sl):
        out = []
        for hi in range(hps):
            hh = grp * hps + hi
            for t in range(t_len):
                for r in range(MOBA_TOPK):
                    for u in range(PPB):
                        phys = pt_ref[bb * n_pages + blk_of(bb, hh, t, r) * PPB + u]
                        i = hi * n_buf + (t * MOBA_TOPK + r) * PPB + u
                        out.append(pltpu.make_async_copy(pk_hbm.at[layer, phys, :, hh, :], kbuf.at[sl, i], sem.at[0, sl]))
                        out.append(pltpu.make_async_copy(pv_hbm.at[layer, phys, :, hh, :], vbuf.at[sl, i], sem.at[1, sl]))
        return out

    @pl.when(step == 0)
    def _():
        for cp in copies(b, hg, slot):
            cp.start()

    @pl.when(step + 1 < n_steps)
    def _():
        nxt = step + 1
        for cp in copies(nxt // n_groups, nxt % n_groups, 1 - slot):
            cp.start()

    row = lax.broadcasted_iota(jnp.int32, (t_len, page), 0)
    lane = lax.broadcasted_iota(jnp.int32, (t_len, page), 1)
    qi = lax.broadcasted_iota(jnp.int32, (t_len, t_len), 0)
    ki = lax.broadcasted_iota(jnp.int32, (t_len, t_len), 1)

    def tree(xs, op):
        while len(xs) > 1:
            xs = [op(xs[n], xs[n + 1]) for n in range(0, len(xs) - 1, 2)] + ([xs[-1]] if len(xs) % 2 else [])
        return xs[0]

    for i in range(hps * n_buf):
        pltpu.make_async_copy(pk_hbm.at[layer, 0, :, 0, :], kbuf.at[slot, i], sem.at[0, slot]).wait()
        pltpu.make_async_copy(pv_hbm.at[layer, 0, :, 0, :], vbuf.at[slot, i], sem.at[1, slot]).wait()

    his = range(hps)
    cols = [slice(hi * dh, (hi + 1) * dh) for hi in his]
    slopes = [sl_ref[hi][:, :page] for hi in his]
    qs = [(q_ref[0, :, cols[hi]] * (dh ** -0.5)).astype(BF16) for hi in his]
    s_owns = [jnp.where(ki <= qi, _dot_nt(qs[hi], kn_ref[0, :, cols[hi]].astype(BF16))
                        - slopes[hi][:, :t_len] * (qi - ki).astype(F32), NEG) for hi in his]
    scores = [[] for _ in his]
    for t in range(t_len):
        for r in range(MOBA_TOPK):
            for u in range(PPB):
                for hi in his:
                    dist = (past + row) - (blk_of(b, hg * hps + hi, t, r) * MOBA_BLOCK + u * page + lane)
                    s = _dot_nt(qs[hi], kbuf[slot, hi * n_buf + (t * MOBA_TOPK + r) * PPB + u].astype(BF16))
                    scores[hi].append(jnp.where(row == t, s - slopes[hi] * dist.astype(F32), NEG))
    ms = [jnp.maximum(s_owns[hi].max(axis=-1, keepdims=True),
                      tree(scores[hi], jnp.maximum).max(axis=-1, keepdims=True)) for hi in his]
    p_owns = [jnp.exp(s_owns[hi] - ms[hi]) for hi in his]
    ps = [[jnp.exp(s - ms[hi]) for s in scores[hi]] for hi in his]
    inv_ls = [1.0 / (p_owns[hi].sum(axis=-1, keepdims=True) + tree(ps[hi], jnp.add).sum(axis=-1, keepdims=True))
              for hi in his]
    accs = [_dot((p_owns[hi] * inv_ls[hi]).astype(BF16), vn_ref[0, :, cols[hi]].astype(BF16)) for hi in his]
    for i in range(n_buf):
        for hi in his:
            accs[hi] = accs[hi] + _dot((ps[hi][i] * inv_ls[hi]).astype(BF16),
                                       vbuf[slot, hi * n_buf + i].astype(BF16))
    for hi in his:
        o_ref[0, hi] = accs[hi]


def moba_decode_attn(q, k_new, v_new, pool_k, pool_v, layer, page_table, sel):
    n_seq, t_len, w = q.shape
    n_pages = page_table.shape[1]
    hh, dh = MOBA_HEADS, MOBA_DH
    _, _, page, _, _ = pool_k.shape
    past = n_pages * page
    assert past % MOBA_BLOCK == 0
    hps = 2 if hh % 2 == 0 else 1
    n_buf = hps * t_len * MOBA_TOPK * PPB
    row = lambda: pl.BlockSpec((1, t_len, hps * dh), lambda b, g, pt, sl: (b, 0, g))
    kern = functools.partial(_moba_dec_kernel, layer=layer, t_len=t_len, past=past, n_pages=n_pages,
                             n_heads=hh, hps=hps, n_steps=n_seq * (hh // hps))
    return pl.pallas_call(
        kern,
        grid_spec=pltpu.PrefetchScalarGridSpec(
            num_scalar_prefetch=2, grid=(n_seq, hh // hps),
            in_specs=[row(), row(), row(),
                      pl.BlockSpec((hps, 1, MOBA_BLOCK), lambda b, g, pt, sl: (g, 0, 0)),
                      pl.BlockSpec(memory_space=pl.ANY), pl.BlockSpec(memory_space=pl.ANY)],
            out_specs=pl.BlockSpec((1, hps, t_len, dh), lambda b, g, pt, sl: (b, g, 0, 0)),
            scratch_shapes=[pltpu.VMEM((2, n_buf, page, dh), F32), pltpu.VMEM((2, n_buf, page, dh), F32),
                            pltpu.SemaphoreType.DMA((2, 2))]),
        out_shape=jax.ShapeDtypeStruct((n_seq, hh, t_len, dh), F32),
        compiler_params=_cparams(("arbitrary", "arbitrary")),
        name="moba_decode_attn",
    )(page_table.reshape(-1), sel.reshape(-1), q, k_new, v_new, _alibi_rows(hh, MOBA_BLOCK), pool_k, pool_v)


def _last_rows(buf, x, n):
    t = x.shape[1]
    if t >= n:
        return x[:, t - n:]
    return jnp.concatenate([buf[:, buf.shape[1] - (n - t):], x], axis=1)


def _pad_front(buf, rows):
    return jnp.pad(buf, ((0, 0), (rows - buf.shape[1], 0), (0, 0)))


def kernel(x_prompt, x_sample, state_dn_s, state_dn_conv, cache_moba_k, cache_moba_v, page_table, cache_mem_k, cache_mem_v, state_ffn_conv, mem_prompt, norm_mix, norm_xattn, norm_mem, norm_ffn, norm_final, dn_w_in, dn_conv_w, dn_a_log, dn_dt_bias, dn_o_gain, dn_w_out, moba_w_qkv, moba_w_out, xa_w_q, xa_w_kv, xa_w_out, ffn_w_up, ffn_conv_w, ffn_w_down):
    bp, seq, d = x_prompt.shape
    bs, dseq, _ = x_sample.shape
    depth = norm_mix.shape[0]
    mem_len = mem_prompt.shape[1]
    d_ff = ffn_w_down.shape[1]
    qk_w = DN_HEADS * DN_DK
    v_w = DN_HEADS * DN_DV
    conv_ch = 2 * qk_w + v_w
    moba_w = MOBA_HEADS * MOBA_DH
    xa_w = XA_HEADS * XA_DH
    bf = lambda a: a.astype(BF16)

    hp = x_prompt.reshape(bp * seq, d)
    hs = x_sample.reshape(bs * dseq, d)
    outs = {k: [] for k in ("p_dn_s", "p_dn_c", "p_mk", "p_mv", "p_memk", "p_memv", "p_ffc",
                            "s_dn_s", "s_dn_c", "s_mk", "s_mv", "s_ffc")}
    kmeans = {}
    for layer in range(depth):
        if layer % 2 == 0:
            ia = layer // 2
            w_in = dn_w_in[ia]
            ws = [bf(w_in[:, :conv_ch]), bf(w_in[:, conv_ch:conv_ch + v_w]),
                  bf(jnp.pad(w_in[:, conv_ch + v_w:], ((0, 0), (0, LANES - 2 * DN_HEADS))))]
            w_out = bf(dn_w_out[ia])
            args = (dn_conv_w[ia], dn_a_log[ia], dn_dt_bias[ia], dn_o_gain[ia])
            qkv, z, ba = norm_proj(hp, norm_mix[layer], ws, tm=512)
            qkv3 = qkv.reshape(bp, seq, conv_ch)
            tt, hps = 512, 4
            n_steps = bp * (DN_HEADS // hps) * (seq // tt)
            n_pg_all = page_table.shape[0] * page_table.shape[1]
            ride = (layer + 1 < depth and n_pg_all % n_steps == 0 and (n_pg_all // n_steps) % PPB == 0
                    and page_table.shape[1] % (n_pg_all // n_steps) == 0)
            o, s_fin, km = deltanet_core(qkv3, z.reshape(bp, seq, v_w), ba.reshape(bp, seq, LANES),
                                         jnp.zeros((bp, SUBLANES, conv_ch), F32),
                                         jnp.zeros((bp, DN_HEADS, DN_DK, DN_DV), F32), *args,
                                         tt=tt, c=DN_CHUNK, t_valid=seq, hps=hps,
                                         **(dict(pool=cache_moba_k, pool_layer=ia, page_table=page_table)
                                            if ride else {}))
            kmeans[ia] = km
            ap = o.reshape(bp * seq, v_w)
            outs["p_dn_s"].append(s_fin)
            outs["p_dn_c"].append(_last_rows(jnp.zeros((bp, DN_CONV - 1, conv_ch), F32), qkv3, DN_CONV - 1))
            qkv, z, ba = norm_proj(hs, norm_mix[layer], ws, tm=256)
            qkv3 = qkv.reshape(bs, dseq, conv_ch)
            tpad = 16
            padt = lambda a: jnp.pad(a.reshape(bs, dseq, -1), ((0, 0), (0, tpad - dseq), (0, 0)))
            o, s_fin, _ = deltanet_core(padt(qkv), padt(z), padt(ba),
                                     _pad_front(state_dn_conv[ia], SUBLANES), state_dn_s[ia], *args,
                                     tt=tpad, c=tpad, t_valid=dseq, hps=DN_HEADS)
            a_s = o[:, :dseq].reshape(bs * dseq, v_w)
            outs["s_dn_s"].append(s_fin)
            outs["s_dn_c"].append(_last_rows(state_dn_conv[ia], qkv3, DN_CONV - 1))
        else:
            ib = layer // 2
            w_qkv = moba_w_qkv[ib]
            ws = [bf(w_qkv[:, j * moba_w:(j + 1) * moba_w]) for j in range(3)]
            w_out = bf(moba_w_out[ib])
            q, k, v = (a.reshape(bp, seq, moba_w) for a in norm_proj(hp, norm_mix[layer], ws, tm=512))
            o = moba_prompt_attn(q, k, v)
            ap = o.reshape(bp * seq, moba_w)
            outs["p_mk"].append(k.reshape(bp, seq, MOBA_HEADS, MOBA_DH))
            outs["p_mv"].append(v.reshape(bp, seq, MOBA_HEADS, MOBA_DH))
            q, k, v = (a.reshape(bs, dseq, moba_w) for a in norm_proj(hs, norm_mix[layer], ws, tm=256))
            kmean = kmeans.get(ib)
            if kmean is None:
                kmean = moba_block_means(cache_moba_k, ib, page_table)
            pick = moba_pick_blocks(q, kmean.reshape(bs, -1, moba_w))
            sel = pick[:, :MOBA_TOPK].reshape(bs, MOBA_TOPK, MOBA_HEADS, dseq)
            o = moba_decode_attn(q, k, v, cache_moba_k, cache_moba_v, ib, page_table, sel)
            o = o.transpose(0, 2, 1, 3).reshape(bs * dseq, moba_w)
            a_s = o
            outs["s_mk"].append(k.reshape(bs, dseq, MOBA_HEADS, MOBA_DH))
            outs["s_mv"].append(v.reshape(bs, dseq, MOBA_HEADS, MOBA_DH))

        w_kv = xa_w_kv[layer]
        mk, mv = norm_proj(mem_prompt.reshape(bp * mem_len, d), norm_mem[layer],
                           [bf(w_kv[:, :xa_w]), bf(w_kv[:, xa_w:])], tm=256)
        mk = mk.reshape(bp, mem_len, xa_w)
        mv = mv.reshape(bp, mem_len, xa_w)
        outs["p_memk"].append(mk.reshape(bp, mem_len, XA_HEADS, XA_DH))
        outs["p_memv"].append(mv.reshape(bp, mem_len, XA_HEADS, XA_DH))
        w_q, w_o = bf(xa_w_q[layer]), bf(xa_w_out[layer])
        hp = mem_xattn(hp, ap, w_out, mk, mv, norm_xattn[layer], w_q, w_o, nb=1, tm=512)
        hs = mem_xattn(hs, a_s, w_out, cache_mem_k[layer].reshape(bs, mem_len, xa_w),
                       cache_mem_v[layer].reshape(bs, mem_len, xa_w),
                       norm_xattn[layer], w_q, w_o, nb=8, tm=dseq)

        last = layer == depth - 1
        w_up, w_dn = bf(ffn_w_up[layer]), bf(ffn_w_down[layer])
        hp3, tail = conv_ffn(hp.reshape(bp, seq, d), jnp.zeros((bp, SUBLANES, 2 * d_ff), F32),
                             norm_ffn[layer], w_up, ffn_conv_w[layer], w_dn, norm_final,
                             tm=512, stride=1, final_norm=last)
        hp = hp3.reshape(bp * seq, d)
        outs["p_ffc"].append(tail[:, SUBLANES - (FFN_CONV - 1):])
        hs_tm = hs.reshape(bs, dseq, d).transpose(1, 0, 2).reshape(1, dseq * bs, d)
        buf_tm = state_ffn_conv[layer].transpose(1, 0, 2).reshape(1, (FFN_CONV - 1) * bs, 2 * d_ff)
        hs3, tail = conv_ffn(hs_tm, buf_tm, norm_ffn[layer], w_up, ffn_conv_w[layer], w_dn, norm_final,
                             tm=dseq * bs, stride=bs, final_norm=last)
        hs = hs3.reshape(dseq, bs, d).transpose(1, 0, 2).reshape(bs * dseq, d)
        outs["s_ffc"].append(tail.reshape(FFN_CONV - 1, bs, 2 * d_ff).transpose(1, 0, 2))

    y_prompt = hp.reshape(bp, seq, d)
    y_sample = hs.reshape(bs, dseq, d)
    st = lambda k: outs[k][0][None] if len(outs[k]) == 1 else jnp.stack(outs[k])
    return (y_prompt, y_sample, st("p_dn_s"), st("p_dn_c"), st("p_mk"), st("p_mv"),
            st("p_memk"), st("p_memv"), st("p_ffc"),
            st("s_dn_s"), st("s_dn_c"), st("s_mk"), st("s_mv"), st("s_ffc"))
```
